```python
import math
import jax, jax.numpy as jnp
from jax import lax
import numpy as np

D_MODEL = 4096
BATCH = 1
SEQ = 8192
DEPTH = 2

N_EVEN = (DEPTH + 1) // 2
N_ODD = DEPTH // 2

GDN_HEAD_DIM = 128
GDN_WIDTH = D_MODEL // 2
GDN_HEADS = GDN_WIDTH // GDN_HEAD_DIM
CONV_WIDTH = 4
GDN_CHUNK = 64
FOX_HEAD_DIM = 128
FOX_WIDTH = D_MODEL - GDN_WIDTH
FOX_HEADS = FOX_WIDTH // FOX_HEAD_DIM
FOX_BLOCK = 128
MIX_WIDTH = GDN_WIDTH + FOX_WIDTH
PROJ_SIZES = (GDN_WIDTH, GDN_WIDTH, GDN_WIDTH, GDN_WIDTH, GDN_HEADS, GDN_HEADS,
              FOX_WIDTH, FOX_WIDTH, FOX_WIDTH, FOX_HEADS)
PROJ_TOTAL = 4 * GDN_WIDTH + 2 * GDN_HEADS + 3 * FOX_WIDTH + FOX_HEADS
S5_GROUP = 16
S5_GROUPS = D_MODEL // S5_GROUP
S5_STATE = 64
S5_CHUNK = 128
S5_MAX_RE = -1e-4
FFN_HIDDEN = -(-8 * D_MODEL // (3 * 256)) * 256
RMS_EPS = 1e-6
L2_EPS = 1e-6

kernel_name = "hybrid_gdn_fox_s5_sandwich"


def _rmsnorm(x, g):
    xf = x.astype(jnp.float32)
    y = xf * lax.rsqrt(jnp.mean(xf * xf, axis=-1, keepdims=True) + RMS_EPS)
    return (y * g.astype(jnp.float32)).astype(x.dtype)


def _l2norm(x):
    return x * lax.rsqrt(jnp.sum(x * x, axis=-1, keepdims=True) + L2_EPS)


def _causal_dwconv(x, w):
    k, c = w.shape
    return lax.conv_general_dilated(
        x, w[:, None, :].astype(x.dtype), window_strides=(1,), padding=[(k - 1, 0)],
        dimension_numbers=('NWC', 'WIO', 'NWC'), feature_group_count=c)


def _gated_delta_rule(q, k, v, g, beta):
    bsz, t, h, dk = q.shape
    dv = v.shape[-1]
    c = GDN_CHUNK
    n = t // c
    q = q * (dk ** -0.5)

    def chunks(a):
        return a.reshape(bsz, n, c, h, -1).transpose(0, 3, 1, 2, 4)

    q, k, v = chunks(q), chunks(k), chunks(v)
    g = jnp.cumsum(g.reshape(bsz, n, c, h).transpose(0, 3, 1, 2), axis=-1)
    beta = beta.reshape(bsz, n, c, h).transpose(0, 3, 1, 2)
    tri = jnp.tril(jnp.ones((c, c), dtype=bool))
    strict = jnp.tril(jnp.ones((c, c), dtype=bool), -1)
    decay = jnp.exp(jnp.where(tri, g[..., :, None] - g[..., None, :], -jnp.inf))
    kb = k * beta[..., None]
    vb = v * beta[..., None]
    lower = jnp.where(strict, jnp.einsum('bhnid,bhnjd->bhnij', kb, k) * decay, 0.0)
    eye = jnp.eye(c, dtype=q.dtype)
    rhs = jnp.concatenate([vb, kb * jnp.exp(g)[..., None]], axis=-1)
    sol = lax.linalg.triangular_solve(eye + lower, rhs, left_side=True, lower=True,
                                      unit_diagonal=True)
    u, w = sol[..., :dv], sol[..., dv:]
    attn = jnp.einsum('bhnid,bhnjd->bhnij', q, k) * decay

    def step(state, inp):
        qc, kc, uc, wc, gc, ac = inp
        v_new = uc - jnp.einsum('bhck,bhkv->bhcv', wc, state)
        o = (jnp.einsum('bhck,bhkv->bhcv', qc * jnp.exp(gc)[..., None], state)
             + jnp.einsum('bhcs,bhsv->bhcv', ac, v_new))
        g_last = gc[..., -1]
        state = (state * jnp.exp(g_last)[..., None, None]
                 + jnp.einsum('bhck,bhcv->bhkv',
                              kc * jnp.exp(g_last[..., None] - gc)[..., None], v_new))
        return state, o

    xs = tuple(jnp.moveaxis(a, 2, 0) for a in (q, k, u, w, g, attn))
    s0 = jnp.zeros((bsz, h, dk, dv), dtype=q.dtype)
    _, o = lax.scan(step, s0, xs)
    return o.transpose(1, 0, 3, 2, 4).reshape(bsz, t, h, dv)


def _forgetting_attention(q, k, v, f_logit):
    bsz, t, _ = q.shape
    h, d = FOX_HEADS, FOX_HEAD_DIM
    q = q.astype(jnp.float32).reshape(bsz, t, h, d) * (d ** -0.5)
    k = k.astype(jnp.float32).reshape(bsz, t, h, d)
    v = v.astype(jnp.float32).reshape(bsz, t, h, d)
    cum = jnp.cumsum(jax.nn.log_sigmoid(f_logit.astype(jnp.float32)), axis=1)
    cum_k = cum.transpose(0, 2, 1)
    nb = t // FOX_BLOCK
    q_blocks = q.reshape(bsz, nb, FOX_BLOCK, h, d).transpose(1, 0, 2, 3, 4)
    c_blocks = cum_k.reshape(bsz, h, nb, FOX_BLOCK).transpose(2, 0, 1, 3)
    k_pos = jnp.arange(t)

    def block(args):
        qi, ci, bi = args
        q_pos = bi * FOX_BLOCK + jnp.arange(FOX_BLOCK)
        s = jnp.einsum('bqhd,bkhd->bhqk', qi, k) + (ci[..., :, None] - cum_k[..., None, :])
        s = jnp.where(k_pos[None, :] <= q_pos[:, None], s, -jnp.inf)
        p = jax.nn.softmax(s, axis=-1)
        return jnp.einsum('bhqk,bkhd->bqhd', p, v)

    o = lax.map(block, (q_blocks, c_blocks, jnp.arange(nb)))
    return o.transpose(1, 0, 2, 3, 4).reshape(bsz, t, h * d)


def _even_mixer(hn, w_in, conv_w, a_log, dt_bias, o_norm, f_bias, w_out):
    bsz, t, _ = hn.shape
    f32 = jnp.float32
    proj = hn @ w_in
    split_at = np.cumsum(PROJ_SIZES)[:-1].tolist()
    aq, ak, av, az, aa, ab, fq, fk, fv, ff = jnp.split(proj, split_at, axis=-1)
    qkv = jax.nn.silu(_causal_dwconv(jnp.concatenate([aq, ak, av], axis=-1), conv_w)).astype(f32)
    aq, ak, av = jnp.split(qkv, [GDN_WIDTH, 2 * GDN_WIDTH], axis=-1)
    heads = (bsz, t, GDN_HEADS, GDN_HEAD_DIM)
    aq = _l2norm(aq.reshape(heads))
    ak = _l2norm(ak.reshape(heads))
    av = av.reshape(heads)
    g = -jnp.exp(a_log.astype(f32)) * jax.nn.softplus(aa.astype(f32) + dt_bias.astype(f32))
    beta = jax.nn.sigmoid(ab.astype(f32))
    o_a = _gated_delta_rule(aq, ak, av, g, beta)
    o_a = _rmsnorm(o_a, o_norm) * jax.nn.silu(az.astype(f32).reshape(heads))
    o_b = _forgetting_attention(fq, fk, fv, ff + f_bias)
    o = jnp.concatenate([o_a.reshape(bsz, t, GDN_WIDTH), o_b], axis=-1).astype(hn.dtype)
    return o @ w_out


def _s5_mixer(hn, a_re, a_im, log_step, b_re, b_im, c_re, c_im, d_skip, w_glu_a, w_glu_b):
    bsz, t, d = hn.shape
    f32 = jnp.float32
    u = hn.astype(f32)
    lam_re = jnp.minimum(a_re.astype(f32), S5_MAX_RE)
    lam_im = a_im.astype(f32)
    dt = jnp.exp(log_step.astype(f32))[:, None]
    mag = jnp.exp(lam_re * dt)
    ab_re = mag * jnp.cos(lam_im * dt)
    ab_im = mag * jnp.sin(lam_im * dt)
    nr, ni = ab_re - 1.0, ab_im
    den = lam_re * lam_re + lam_im * lam_im
    f_re = (nr * lam_re + ni * lam_im) / den
    f_im = (ni * lam_re - nr * lam_im) / den
    b_re, b_im = b_re.astype(f32), b_im.astype(f32)
    bb_re = f_re[..., None] * b_re - f_im[..., None] * b_im
    bb_im = f_re[..., None] * b_im + f_im[..., None] * b_re
    c_re, c_im = c_re.astype(f32), c_im.astype(f32)
    L = S5_CHUNK
    nc = t // L
    u_chunks = u.reshape(bsz, nc, L, S5_GROUPS, S5_GROUP).transpose(1, 2, 0, 3, 4)
    a_seq_re = jnp.broadcast_to(ab_re, (L, bsz, S5_GROUPS, S5_STATE))
    a_seq_im = jnp.broadcast_to(ab_im, (L, bsz, S5_GROUPS, S5_STATE))

    def combine(e1, e2):
        a1r, a1i, b1r, b1i = e1
        a2r, a2i, b2r, b2i = e2
        return (a2r * a1r - a2i * a1i, a2r * a1i + a2i * a1r,
                a2r * b1r - a2i * b1i + b2r, a2r * b1i + a2i * b1r + b2i)

    def step(carry, uc):
        sr0, si0 = carry
        bu_r = jnp.einsum('lbgp,gnp->lbgn', uc, bb_re)
        bu_i = jnp.einsum('lbgp,gnp->lbgn', uc, bb_im)
        pr, pi_, xr, xi = lax.associative_scan(combine, (a_seq_re, a_seq_im, bu_r, bu_i), axis=0)
        sr = xr + pr * sr0 - pi_ * si0
        si = xi + pr * si0 + pi_ * sr0
        y = jnp.einsum('gpn,lbgn->lbgp', c_re, sr) - jnp.einsum('gpn,lbgn->lbgp', c_im, si)
        return (sr[-1], si[-1]), y.reshape(L, bsz, d)

    s0 = jnp.zeros((bsz, S5_GROUPS, S5_STATE), dtype=f32)
    _, ys = lax.scan(step, (s0, s0), u_chunks)
    y = ys.transpose(2, 0, 1, 3).reshape(bsz, t, d) + d_skip.astype(f32) * u
    y = jax.nn.gelu(y).astype(hn.dtype)
    return (y @ w_glu_a) * jax.nn.sigmoid(y @ w_glu_b)


def _swiglu(h, w_gate, w_up, w_down):
    return (jax.nn.silu(h @ w_gate) * (h @ w_up)) @ w_down


def setup_inputs(seed: int = 0) -> dict:
    key = jax.random.key(seed)
    ks = jax.random.split(key, 24)
    f32 = jnp.float32
    nrm = jax.random.normal
    uni = jax.random.uniform
    d = D_MODEL
    ne, no = N_EVEN, N_ODD
    g, n, p = S5_GROUPS, S5_STATE, S5_GROUP
    x = nrm(ks[0], (BATCH, SEQ, d), f32)
    norm_g = 1.0 + 0.02 * nrm(ks[1], (DEPTH, 4, d), f32)
    w_in = nrm(ks[2], (ne, d, PROJ_TOTAL), f32) * d ** -0.5
    gdn_conv_w = nrm(ks[3], (ne, CONV_WIDTH, 3 * GDN_WIDTH), f32) * CONV_WIDTH ** -0.5
    gdn_A_log = jnp.log(uni(ks[4], (ne, GDN_HEADS), f32, 1.0, 16.0))
    dt0 = jnp.exp(uni(ks[5], (ne, GDN_HEADS), f32, math.log(1e-3), math.log(1e-1)))
    gdn_dt_bias = dt0 + jnp.log(-jnp.expm1(-dt0))
    gdn_o_norm = 1.0 + 0.02 * nrm(ks[6], (ne, GDN_HEAD_DIM), f32)
    fox_f_bias = uni(ks[7], (ne, FOX_HEADS), f32, 1.0, 3.0)
    w_out = nrm(ks[8], (ne, MIX_WIDTH, d), f32) * MIX_WIDTH ** -0.5
    s5_A_re = -0.5 + 0.01 * nrm(ks[9], (no, g, n), f32)
    s5_A_im = math.pi * jnp.arange(n, dtype=f32)[None, None, :] + 0.01 * nrm(ks[10], (no, g, n), f32)
    s5_log_step = uni(ks[11], (no, g), f32, math.log(1e-3), math.log(1e-1))
    s5_B_re = nrm(ks[12], (no, g, n, p), f32) * (2 * p) ** -0.5
    s5_B_im = nrm(ks[13], (no, g, n, p), f32) * (2 * p) ** -0.5
    s5_C_re = nrm(ks[14], (no, g, p, n), f32) * (2 * n) ** -0.5
    s5_C_im = nrm(ks[15], (no, g, p, n), f32) * (2 * n) ** -0.5
    s5_D = nrm(ks[16], (no, d), f32)
    s5_w_glu_a = nrm(ks[17], (no, d, d), f32) * d ** -0.5
    s5_w_glu_b = nrm(ks[18], (no, d, d), f32) * d ** -0.5
    ffn_w_gate = nrm(ks[19], (DEPTH, d, FFN_HIDDEN), f32) * d ** -0.5
    ffn_w_up = nrm(ks[20], (DEPTH, d, FFN_HIDDEN), f32) * d ** -0.5
    ffn_w_down = nrm(ks[21], (DEPTH, FFN_HIDDEN, d), f32) * FFN_HIDDEN ** -0.5
    return {"x": x, "norm_g": norm_g, "w_in": w_in, "gdn_conv_w": gdn_conv_w,
            "gdn_A_log": gdn_A_log, "gdn_dt_bias": gdn_dt_bias, "gdn_o_norm": gdn_o_norm,
            "fox_f_bias": fox_f_bias, "w_out": w_out, "s5_A_re": s5_A_re, "s5_A_im": s5_A_im,
            "s5_log_step": s5_log_step, "s5_B_re": s5_B_re, "s5_B_im": s5_B_im,
            "s5_C_re": s5_C_re, "s5_C_im": s5_C_im, "s5_D": s5_D, "s5_w_glu_a": s5_w_glu_a,
            "s5_w_glu_b": s5_w_glu_b, "ffn_w_gate": ffn_w_gate, "ffn_w_up": ffn_w_up,
            "ffn_w_down": ffn_w_down}


def reference(x, norm_g, w_in, gdn_conv_w, gdn_A_log, gdn_dt_bias, gdn_o_norm, fox_f_bias,
              w_out, s5_A_re, s5_A_im, s5_log_step, s5_B_re, s5_B_im, s5_C_re, s5_C_im, s5_D,
              s5_w_glu_a, s5_w_glu_b, ffn_w_gate, ffn_w_up, ffn_w_down):
    for layer in range(DEPTH):
        gains = norm_g[layer]
        i = layer // 2
        hn = _rmsnorm(x, gains[0])
        if layer % 2 == 0:
            m = _even_mixer(hn, w_in[i], gdn_conv_w[i], gdn_A_log[i], gdn_dt_bias[i],
                            gdn_o_norm[i], fox_f_bias[i], w_out[i])
        else:
            m = _s5_mixer(hn, s5_A_re[i], s5_A_im[i], s5_log_step[i], s5_B_re[i], s5_B_im[i],
                          s5_C_re[i], s5_C_im[i], s5_D[i], s5_w_glu_a[i], s5_w_glu_b[i])
        x = x + _rmsnorm(m.astype(x.dtype), gains[1])
        f = _swiglu(_rmsnorm(x, gains[2]), ffn_w_gate[layer], ffn_w_up[layer], ffn_w_down[layer])
        x = x + _rmsnorm(f.astype(x.dtype), gains[3])
    return x
```

```python
import functools
import math

import jax
import jax.numpy as jnp
from jax import lax
from jax.experimental import pallas as pl
from jax.experimental.pallas import tpu as pltpu

F32 = jnp.float32
BF16 = jnp.bfloat16

LANES = 128
SUBLANES = 8
VMEM_LIMIT_BYTES = 56 * 1024 * 1024

HEAD_DIM = 128
GDN_HEADS = 16
FOX_HEADS = 16
CONV_WIDTH = 4
GDN_CHUNK = 128
S5_GROUP = 16
S5_STATE = 64
S5_CHUNK = 16
S5_MAX_RE = -1e-4
RMS_EPS = 1e-6
L2_EPS = 1e-6
GATE_LANES = LANES
NEG_INF = float("-inf")


def _params(*semantics):
    return pltpu.CompilerParams(dimension_semantics=semantics,
                                vmem_limit_bytes=VMEM_LIMIT_BYTES)


def _silu(x):
    return x * jax.nn.sigmoid(x)


def _softplus(x):
    return jnp.maximum(x, 0.0) + jnp.log1p(jnp.exp(-jnp.abs(x)))


def _dot(a, b):
    return jnp.dot(a, b, preferred_element_type=F32)


def _dot_nt(a, b):
    return lax.dot_general(a, b, (((1,), (1,)), ((), ())), preferred_element_type=F32)


def _dot_nt_f32(a, b):
    return lax.dot_general(a, b, (((1,), (1,)), ((), ())), precision=lax.Precision.HIGHEST,
                           preferred_element_type=F32)


def _dot_tn(a, b):
    return lax.dot_general(a, b, (((0,), (0,)), ((), ())), preferred_element_type=F32)


def _split2(x):
    hi = x.astype(BF16)
    lo = (x - hi.astype(F32)).astype(BF16)
    return hi, lo


def _dot_split(a, b):
    ah, al = _split2(a)
    bh, bl = _split2(b)
    return _dot(ah, bh) + _dot(al, bh) + _dot(ah, bl)


def _rms(x, g):
    y = x * lax.rsqrt(jnp.mean(x * x, axis=-1, keepdims=True) + RMS_EPS)
    return y * g


def _rmsnorm_kernel(x_ref, g_ref, o_ref):
    o_ref[...] = _rms(x_ref[...], g_ref[...]).astype(o_ref.dtype)


def rmsnorm(x, g, out_dtype, rows=256):
    t, d = x.shape
    return pl.pallas_call(
        _rmsnorm_kernel,
        grid=(t // rows,),
        in_specs=[pl.BlockSpec((rows, d), lambda i: (i, 0)),
                  pl.BlockSpec((1, d), lambda i: (0, 0))],
        out_specs=pl.BlockSpec((rows, d), lambda i: (i, 0)),
        out_shape=jax.ShapeDtypeStruct((t, d), out_dtype),
        compiler_params=_params("parallel"),
        name="rmsnorm",
    )(x, g.reshape(1, d))


def _resid_norm_kernel(x_ref, m_ref, gp_ref, gn_ref, xo_ref, ho_ref):
    xn = x_ref[...] + _rms(m_ref[...].astype(F32), gp_ref[...])
    xo_ref[...] = xn
    ho_ref[...] = _rms(xn, gn_ref[...]).astype(ho_ref.dtype)


def _resid_kernel(x_ref, m_ref, gp_ref, xo_ref):
    xo_ref[...] = x_ref[...] + _rms(m_ref[...].astype(F32), gp_ref[...])


def resid_norm(x, m, g_post, g_next, next_dtype, rows=256):
    t, d = x.shape
    row_spec = pl.BlockSpec((rows, d), lambda i: (i, 0))
    vec_spec = pl.BlockSpec((1, d), lambda i: (0, 0))
    if g_next is None:
        return pl.pallas_call(
            _resid_kernel, grid=(t // rows,),
            in_specs=[row_spec, row_spec, vec_spec], out_specs=row_spec,
            out_shape=jax.ShapeDtypeStruct((t, d), F32),
            compiler_params=_params("parallel"), name="resid",
        )(x, m, g_post.reshape(1, d))
    return pl.pallas_call(
        _resid_norm_kernel, grid=(t // rows,),
        in_specs=[row_spec, row_spec, vec_spec, vec_spec],
        out_specs=[row_spec, row_spec],
        out_shape=[jax.ShapeDtypeStruct((t, d), F32), jax.ShapeDtypeStruct((t, d), next_dtype)],
        compiler_params=_params("parallel"), name="resid_norm",
    )(x, m, g_post.reshape(1, d), g_next.reshape(1, d))


def _mm_kernel(a_ref, w_ref, o_ref):
    o_ref[...] = _dot(a_ref[...], w_ref[...]).astype(o_ref.dtype)


def _mm_swiglu_kernel(a_ref, wg_ref, wu_ref, o_ref):
    a = a_ref[...]
    o_ref[...] = (_silu(_dot(a, wg_ref[...])) * _dot(a, wu_ref[...])).astype(o_ref.dtype)


def _mm_glu_kernel(a_ref, wa_ref, wb_ref, o_ref):
    a = a_ref[...]
    o_ref[...] = (_dot(a, wa_ref[...]) * jax.nn.sigmoid(_dot(a, wb_ref[...]))).astype(o_ref.dtype)


def _matmul_call(body, a, weights, out_dtype, tm, tn, name):
    m, k = a.shape
    n = weights[0].shape[1]
    w_spec = pl.BlockSpec((k, tn), lambda i, j: (0, j))
    return pl.pallas_call(
        body,
        grid=(m // tm, n // tn),
        in_specs=[pl.BlockSpec((tm, k), lambda i, j: (i, 0))] + [w_spec] * len(weights),
        out_specs=pl.BlockSpec((tm, tn), lambda i, j: (i, j)),
        out_shape=jax.ShapeDtypeStruct((m, n), out_dtype),
        compiler_params=_params("parallel", "arbitrary"),
        name=name,
    )(a, *weights)


def matmul(a, w, out_dtype, tm, tn, name="matmul"):
    return _matmul_call(_mm_kernel, a, (w,), out_dtype, tm, tn, name)


def matmul_swiglu(a, wg, wu, tm, tn):
    return _matmul_call(_mm_swiglu_kernel, a, (wg, wu), BF16, tm, tn, "ffn_gate_up")


def matmul_glu(a, wa, wb, tm, tn):
    return _matmul_call(_mm_glu_kernel, a, (wa, wb), BF16, tm, tn, "s5_glu")


def _gates_kernel(hn_ref, w_ref, prm_ref, o_ref, carry_ref, *, rows):
    @pl.when(pl.program_id(0) == 0)
    def _():
        carry_ref[...] = jnp.zeros_like(carry_ref)

    p = _dot(hn_ref[...], w_ref[...])
    lane = lax.broadcasted_iota(jnp.int32, (1, GATE_LANES), 1)
    is_decay = lane < GDN_HEADS
    is_beta = (lane >= GDN_HEADS) & (lane < 2 * GDN_HEADS)
    is_forget = (lane >= 2 * GDN_HEADS) & (lane < 2 * GDN_HEADS + FOX_HEADS)
    a_log, dt_bias, f_bias = prm_ref[0:1, :], prm_ref[1:2, :], prm_ref[2:3, :]
    decay = -jnp.exp(a_log) * _softplus(p + dt_bias)
    beta = jax.nn.sigmoid(p)
    log_f = -_softplus(-(p + f_bias))
    vals = jnp.where(is_decay, decay, jnp.where(is_forget, log_f, 0.0))

    r = lax.broadcasted_iota(jnp.int32, (GDN_CHUNK, GDN_CHUNK), 0)
    c = lax.broadcasted_iota(jnp.int32, (GDN_CHUNK, GDN_CHUNK), 1)
    tri = (r >= c).astype(BF16)
    carry = carry_ref[...]
    for ci in range(rows // GDN_CHUNK):
        sl = slice(ci * GDN_CHUNK, (ci + 1) * GDN_CHUNK)
        v = vals[sl]
        hi = v.astype(BF16)
        r1 = v - hi.astype(F32)
        mid = r1.astype(BF16)
        lo = (r1 - mid.astype(F32)).astype(BF16)
        cs = _dot(tri, hi) + _dot(tri, mid) + _dot(tri, lo)
        run = cs + carry
        o_ref[sl, :] = jnp.where(is_decay, cs, jnp.where(is_beta, beta[sl], jnp.where(is_forget, run, 0.0)))
        carry = jnp.where(is_forget, run[GDN_CHUNK - 1:GDN_CHUNK, :], 0.0)
    carry_ref[...] = carry


def gates(hn, w_gates, prm, rows=512):
    t, d = hn.shape
    return pl.pallas_call(
        functools.partial(_gates_kernel, rows=rows),
        grid=(t // rows,),
        in_specs=[pl.BlockSpec((rows, d), lambda i: (i, 0)),
                  pl.BlockSpec((d, GATE_LANES), lambda i: (0, 0)),
                  pl.BlockSpec((SUBLANES, GATE_LANES), lambda i: (0, 0))],
        out_specs=pl.BlockSpec((rows, GATE_LANES), lambda i: (i, 0)),
        out_shape=jax.ShapeDtypeStruct((t, GATE_LANES), F32),
        scratch_shapes=[pltpu.VMEM((1, GATE_LANES), F32)],
        compiler_params=_params("arbitrary"),
        name="gates",
    )(hn, w_gates, prm)


def _unit_lower_inverse(lmat):
    n = lmat.shape[0]
    r = lax.broadcasted_iota(jnp.int32, (n, n), 0)
    c = lax.broadcasted_iota(jnp.int32, (n, n), 1)
    inv = (r == c).astype(F32)
    shift = 0
    while (1 << shift) < n:
        rb = r >> shift
        cb = c >> shift
        joins = ((rb & 1) == 1) & (cb == rb - 1)
        cmat = jnp.where(joins, lmat, 0.0)
        inv = inv - _dot_split(inv, _dot_split(cmat, inv))
        shift += 1
    return inv


def _gdn_kernel(q_ref, k_ref, v_ref, z_ref, qh_ref, kh_ref, vh_ref, cwq_ref, cwk_ref, cwv_ref,
                gcol_ref, grow_ref, onorm_ref, o_ref, s_ref, *, rows):
    h = pl.program_id(0)
    first = pl.program_id(1) == 0

    @pl.when(first)
    def _():
        s_ref[...] = jnp.zeros_like(s_ref)

    def conv_silu(x_ref, halo_ref, w_ref):
        x = x_ref[...]
        halo = jnp.where(first, 0.0, halo_ref[...])
        xe = jnp.concatenate([halo, x], axis=0)
        w = w_ref[...]
        y = w[CONV_WIDTH - 1:CONV_WIDTH, :] * x
        for i in range(CONV_WIDTH - 1):
            off = SUBLANES - (CONV_WIDTH - 1) + i
            y = y + w[i:i + 1, :] * xe[off:off + rows, :]
        return _silu(y)

    def l2norm(x):
        return x * lax.rsqrt(jnp.sum(x * x, axis=-1, keepdims=True) + L2_EPS)

    q = l2norm(conv_silu(q_ref, qh_ref, cwq_ref)) * (HEAD_DIM ** -0.5)
    k = l2norm(conv_silu(k_ref, kh_ref, cwk_ref))
    v = conv_silu(v_ref, vh_ref, cwv_ref)

    lane = lax.broadcasted_iota(jnp.int32, (1, GATE_LANES), 1)
    gates_col = gcol_ref[...]
    g_col = jnp.sum(jnp.where(lane == h, gates_col, 0.0), axis=1, keepdims=True)
    beta = jnp.sum(jnp.where(lane == h + GDN_HEADS, gates_col, 0.0), axis=1, keepdims=True)
    g_row = grow_ref[0]

    n = GDN_CHUNK
    r = lax.broadcasted_iota(jnp.int32, (n, n), 0)
    c = lax.broadcasted_iota(jnp.int32, (n, n), 1)
    state = s_ref[...]
    for ci in range(rows // n):
        sl = slice(ci * n, (ci + 1) * n)
        qc, kc, vc, gc, bc = q[sl], k[sl], v[sl], g_col[sl], beta[sl]
        decay = jnp.exp(jnp.where(r >= c, gc - g_row[:, sl], NEG_INF))
        kb = kc * bc
        kb16, k16 = kb.astype(BF16), kc.astype(BF16)
        lmat = jnp.where(r > c, _dot_nt(kb16, k16) * decay, 0.0)
        inv = _unit_lower_inverse(lmat)
        eg = jnp.exp(gc)
        rhs = jnp.concatenate([vc * bc, kb * eg], axis=1)
        sol = _dot_split(inv, rhs)
        u, w = sol[:, :HEAD_DIM], sol[:, HEAD_DIM:]
        attn = _dot_nt(qc.astype(BF16), k16) * decay
        s16 = state.astype(BF16)
        v_new = u - _dot(w.astype(BF16), s16)
        vn16 = v_new.astype(BF16)
        o = _dot((qc * eg).astype(BF16), s16) + _dot(attn.astype(BF16), vn16)
        g_last = gc[n - 1:n, :]
        state = state * jnp.exp(g_last) + _dot_tn((kc * jnp.exp(g_last - gc)).astype(BF16), vn16)
        o_ref[sl, :] = (_rms(o, onorm_ref[...]) * _silu(z_ref[sl, :])).astype(o_ref.dtype)
    s_ref[...] = state


def gdn_mixer(proj_a, conv_w, gates_col, gates_row, o_norm, rows=256):
    t = proj_a.shape[0]
    nh = GDN_HEADS
    halo_blocks = rows // SUBLANES

    def col(section):
        return pl.BlockSpec((rows, HEAD_DIM), lambda h, i: (i, section * nh + h))

    def halo(section):
        return pl.BlockSpec((SUBLANES, HEAD_DIM),
                            lambda h, i: (jnp.maximum(i * halo_blocks - 1, 0), section * nh + h))

    def cw(section):
        return pl.BlockSpec((CONV_WIDTH, HEAD_DIM), lambda h, i: (0, section * nh + h))

    return pl.pallas_call(
        functools.partial(_gdn_kernel, rows=rows),
        grid=(nh, t // rows),
        in_specs=[col(0), col(1), col(2), col(3), halo(0), halo(1), halo(2), cw(0), cw(1), cw(2),
                  pl.BlockSpec((rows, GATE_LANES), lambda h, i: (i, 0)),
                  pl.BlockSpec((1, 1, rows), lambda h, i: (h, 0, i)),
                  pl.BlockSpec((1, HEAD_DIM), lambda h, i: (0, 0))],
        out_specs=pl.BlockSpec((rows, HEAD_DIM), lambda h, i: (i, h)),
        out_shape=jax.ShapeDtypeStruct((t, nh * HEAD_DIM), BF16),
        scratch_shapes=[pltpu.VMEM((HEAD_DIM, HEAD_DIM), F32)],
        compiler_params=_params("parallel", "arbitrary"),
        name="gdn",
    )(proj_a, proj_a, proj_a, proj_a, proj_a, proj_a, proj_a, conv_w, conv_w, conv_w,
      gates_col, gates_row, o_norm.reshape(1, HEAD_DIM))


def _fox_kernel(q_ref, k_ref, v_ref, gcol_ref, grow_ref, o_ref, m_ref, l_ref, acc_ref, cq_ref, *, tq, tk):
    h = pl.program_id(0)
    qi = pl.program_id(1)
    ki = pl.program_id(2)
    last_k = (qi * tq + tq - 1) // tk

    @pl.when(ki == 0)
    def _():
        m_ref[...] = jnp.full_like(m_ref, NEG_INF)
        l_ref[...] = jnp.zeros_like(l_ref)
        acc_ref[...] = jnp.zeros_like(acc_ref)
        lane = lax.broadcasted_iota(jnp.int32, (1, GATE_LANES), 1)
        cq_ref[...] = jnp.sum(jnp.where(lane == h + 2 * GDN_HEADS, gcol_ref[...], 0.0),
                              axis=1, keepdims=True)

    @pl.when(ki <= last_k)
    def _():
        s = _dot_nt(q_ref[...], k_ref[...]) * (HEAD_DIM ** -0.5)
        s = s + (cq_ref[...] - grow_ref[0])
        q_pos = qi * tq + lax.broadcasted_iota(jnp.int32, (tq, tk), 0)
        k_pos = ki * tk + lax.broadcasted_iota(jnp.int32, (tq, tk), 1)
        s = jnp.where(k_pos <= q_pos, s, NEG_INF)
        m_prev = m_ref[...]
        m_new = jnp.maximum(m_prev, jnp.max(s, axis=1, keepdims=True))
        alpha = jnp.exp(m_prev - m_new)
        p = jnp.exp(s - m_new)
        l_ref[...] = alpha * l_ref[...] + jnp.sum(p, axis=1, keepdims=True)
        acc_ref[...] = alpha * acc_ref[...] + _dot(p.astype(BF16), v_ref[...])
        m_ref[...] = m_new

    @pl.when(ki == last_k)
    def _():
        o_ref[...] = (acc_ref[...] / l_ref[...]).astype(o_ref.dtype)


def fox_attention(proj_b, gates_col, gates_row, tq=512, tk=512):
    t = proj_b.shape[0]
    nh = FOX_HEADS

    def kv_block(h, qi, ki):
        return jnp.minimum(ki, (qi * tq + tq - 1) // tk)

    return pl.pallas_call(
        functools.partial(_fox_kernel, tq=tq, tk=tk),
        grid=(nh, t // tq, t // tk),
        in_specs=[pl.BlockSpec((tq, HEAD_DIM), lambda h, qi, ki: (qi, h)),
                  pl.BlockSpec((tk, HEAD_DIM), lambda h, qi, ki: (kv_block(h, qi, ki), nh + h)),
                  pl.BlockSpec((tk, HEAD_DIM), lambda h, qi, ki: (kv_block(h, qi, ki), 2 * nh + h)),
                  pl.BlockSpec((tq, GATE_LANES), lambda h, qi, ki: (qi, 0)),
                  pl.BlockSpec((1, 1, tk), lambda h, qi, ki: (2 * GDN_HEADS + h, 0, kv_block(h, qi, ki)))],
        out_specs=pl.BlockSpec((tq, HEAD_DIM), lambda h, qi, ki: (qi, h)),
        out_shape=jax.ShapeDtypeStruct((t, nh * HEAD_DIM), BF16),
        scratch_shapes=[pltpu.VMEM((tq, 1), F32), pltpu.VMEM((tq, 1), F32),
                        pltpu.VMEM((tq, HEAD_DIM), F32), pltpu.VMEM((tq, 1), F32)],
        compiler_params=_params("parallel", "parallel", "arbitrary"),
        name="fox",
    )(proj_b, proj_b, proj_b, gates_col, gates_row)


S5_SCAN_LEVELS = 16


def _s5_setup_kernel(are_ref, aim_ref, ls_ref, bre_ref, bim_ref, cre_ref, cim_ref,
                     g_ref, ere_ref, eim_ref, pre_ref, pim_ref, sre_ref, sim_ref, *, gb):
    p, l = S5_GROUP, S5_CHUNK
    lam_re = jnp.minimum(are_ref[...], S5_MAX_RE)
    lam_im = aim_ref[...]
    dt = jnp.exp(ls_ref[...])
    mag = jnp.exp(lam_re * dt)
    a_re = mag * jnp.cos(lam_im * dt)
    a_im = mag * jnp.sin(lam_im * dt)
    nr, ni = a_re - 1.0, a_im
    den = lam_re * lam_re + lam_im * lam_im
    f_re = (nr * lam_re + ni * lam_im) / den
    f_im = (ni * lam_re - nr * lam_im) / den
    b_re, b_im = bre_ref[...], bim_ref[...]
    bb_re = f_re * b_re - f_im * b_im
    bb_im = f_re * b_im + f_im * b_re
    c_re, c_im = cre_ref[...], cim_ref[...]

    pows = [(jnp.ones_like(a_re), jnp.zeros_like(a_im))]
    for _ in range(l):
        pr, pi_ = pows[-1]
        pows.append((pr * a_re - pi_ * a_im, pr * a_im + pi_ * a_re))

    ca_re = [c_re * pr - c_im * pi_ for pr, pi_ in pows]
    ca_im = [c_re * pi_ + c_im * pr for pr, pi_ in pows]
    lo_re = jnp.concatenate(ca_re[:l], axis=1)
    lo_im = jnp.concatenate(ca_im[:l], axis=1)
    for gi in range(gb):
        g_ref[gi] = _dot_nt_f32(lo_re[gi], bb_re[gi]) - _dot_nt_f32(lo_im[gi], bb_im[gi])
    pre_ref[...] = jnp.concatenate(ca_re[1:], axis=1)
    pim_ref[...] = -jnp.concatenate(ca_im[1:], axis=1)
    ere_ref[...] = jnp.concatenate(
        [pows[l - 1 - j][0] * bb_re - pows[l - 1 - j][1] * bb_im for j in range(l)], axis=1)
    eim_ref[...] = jnp.concatenate(
        [pows[l - 1 - j][0] * bb_im + pows[l - 1 - j][1] * bb_re for j in range(l)], axis=1)
    sr, si = pows[l]
    for lv in range(S5_SCAN_LEVELS):
        sre_ref[:, lv:lv + 1, :] = sr
        sim_ref[:, lv:lv + 1, :] = si
        sr, si = sr * sr - si * si, 2.0 * sr * si


def s5_setup(a_re, a_im, log_step, b_re_t, b_im_t, c_re, c_im, gb=8):
    g, n = a_re.shape
    p, l = S5_GROUP, S5_CHUNK
    vec = pl.BlockSpec((gb, 1, n), lambda i: (i, 0, 0))
    mat = pl.BlockSpec((gb, p, n), lambda i: (i, 0, 0))
    big = pl.BlockSpec((gb, l * p, n), lambda i: (i, 0, 0))
    lvl = pl.BlockSpec((gb, S5_SCAN_LEVELS, n), lambda i: (i, 0, 0))
    big_shape = jax.ShapeDtypeStruct((g, l * p, n), F32)
    lvl_shape = jax.ShapeDtypeStruct((g, S5_SCAN_LEVELS, n), F32)
    return pl.pallas_call(
        functools.partial(_s5_setup_kernel, gb=gb),
        grid=(g // gb,),
        in_specs=[vec, vec, pl.BlockSpec((gb, 1, 1), lambda i: (i, 0, 0)), mat, mat, mat, mat],
        out_specs=[pl.BlockSpec((gb, l * p, p), lambda i: (i, 0, 0)), big, big, big, big, lvl, lvl],
        out_shape=[jax.ShapeDtypeStruct((g, l * p, p), F32), big_shape, big_shape, big_shape, big_shape,
                   lvl_shape, lvl_shape],
        compiler_params=_params("parallel"),
        name="s5_setup",
    )(a_re.reshape(g, 1, n), a_im.reshape(g, 1, n), log_step.reshape(g, 1, 1), b_re_t, b_im_t, c_re, c_im)


def _s5_kernel(u_ref, kt_ref, e_ref, p_ref, s1_ref, s2_ref, d_ref, o_ref, *, gb, levels):
    nc = u_ref.shape[1]
    half = S5_STATE
    row = lax.broadcasted_iota(jnp.int32, (nc, 2 * half), 0)
    for gi in range(gb):
        u = u_ref[gi]
        u16 = u.astype(BF16)
        x = _dot(u16, e_ref[gi])
        for lv in range(levels):
            sh = 1 << lv
            xs = jnp.where(row >= sh, pltpu.roll(x, sh, axis=0), 0.0)
            xs_sw = pltpu.roll(xs, half, axis=1)
            x = x + s1_ref[gi, lv:lv + 1, :] * xs + s2_ref[gi, lv:lv + 1, :] * xs_sw
        s_in = jnp.where(row >= 1, pltpu.roll(x, 1, axis=0), 0.0)
        y = _dot(u16, kt_ref[gi]) + _dot_nt(s_in.astype(BF16), p_ref[gi]) + d_ref[gi] * u
        o_ref[gi] = jax.nn.gelu(y).astype(o_ref.dtype)


def s5_scan(u_r, kt, emat, pmat, s1, s2, d_t, gb=4):
    g, nc, w = u_r.shape
    levels = max(1, math.ceil(math.log2(nc)))
    assert levels <= S5_SCAN_LEVELS

    def blk(*shape):
        return pl.BlockSpec((gb,) + shape, lambda i: (i,) + (0,) * len(shape))

    return pl.pallas_call(
        functools.partial(_s5_kernel, gb=gb, levels=levels),
        grid=(g // gb,),
        in_specs=[blk(nc, w), blk(w, w), blk(w, 2 * S5_STATE), blk(w, 2 * S5_STATE),
                  blk(S5_SCAN_LEVELS, 2 * S5_STATE), blk(S5_SCAN_LEVELS, 2 * S5_STATE), blk(1, w)],
        out_specs=blk(nc, w),
        out_shape=jax.ShapeDtypeStruct((g, nc, w), BF16),
        compiler_params=_params("parallel"),
        name="s5_scan",
    )(u_r, kt, emat, pmat, s1, s2, d_t)


def s5_mixer(hn, a_re, a_im, log_step, b_re, b_im, c_re, c_im, d_skip):
    t, d = hn.shape
    g, n = a_re.shape
    p, l = S5_GROUP, S5_CHUNK
    nc = t // l
    gm, e_re, e_im, p_re, p_im, s_re, s_im = s5_setup(
        a_re, a_im, log_step, b_re.transpose(0, 2, 1), b_im.transpose(0, 2, 1), c_re, c_im)
    gm4 = gm.reshape(g, l, p, p)
    jj = jnp.arange(l)[:, None]
    tt = jnp.arange(l)[None, :]
    blocks = jnp.where((tt >= jj)[None, :, :, None, None], gm4[:, jnp.clip(tt - jj, 0, l - 1)], 0.0)
    kt = blocks.transpose(0, 1, 4, 2, 3).reshape(g, l * p, l * p).astype(BF16)
    emat = jnp.concatenate([e_re, e_im], axis=-1).astype(BF16)
    pmat = jnp.concatenate([p_re, p_im], axis=-1).astype(BF16)
    s1 = jnp.concatenate([s_re, s_re], axis=-1)
    s2 = jnp.concatenate([-s_im, s_im], axis=-1)
    d_t = jnp.tile(d_skip.reshape(g, 1, p), (1, 1, l))
    u_r = hn.reshape(nc, l, g, p).transpose(2, 0, 1, 3).reshape(g, nc, l * p)
    y_r = s5_scan(u_r, kt, emat, pmat, s1, s2, d_t)
    return y_r.reshape(g, nc, l, p).transpose(1, 2, 0, 3).reshape(t, d)


def _ffn(hn, w_gate, w_up, w_down):
    hidden = matmul_swiglu(hn, w_gate.astype(BF16), w_up.astype(BF16), tm=1024, tn=256)
    return matmul(hidden, w_down.astype(BF16), BF16, tm=512, tn=256, name="ffn_down")


def kernel(x, norm_g, w_in, gdn_conv_w, gdn_A_log, gdn_dt_bias, gdn_o_norm, fox_f_bias, w_out, s5_A_re, s5_A_im, s5_log_step, s5_B_re, s5_B_im, s5_C_re, s5_C_im, s5_D, s5_w_glu_a, s5_w_glu_b, ffn_w_gate, ffn_w_up, ffn_w_down):
    bsz, t, d = x.shape
    assert bsz == 1
    x = x.reshape(t, d)
    gw = GDN_HEADS * HEAD_DIM
    fw = FOX_HEADS * HEAD_DIM
    tm = min(1024, t)

    w0 = w_in[0]
    o_gate = 4 * gw
    o_fox = o_gate + 2 * GDN_HEADS
    o_forget = o_fox + 3 * fw
    w_a = w0[:, :o_gate].astype(BF16)
    w_b = w0[:, o_fox:o_forget].astype(BF16)
    pad = GATE_LANES - 2 * GDN_HEADS - FOX_HEADS
    w_g = jnp.concatenate([w0[:, o_gate:o_fox], w0[:, o_forget:], jnp.zeros((d, pad), F32)], axis=1).astype(BF16)
    prm = jnp.zeros((SUBLANES, GATE_LANES), F32)
    prm = prm.at[0, :GDN_HEADS].set(gdn_A_log[0]).at[1, :GDN_HEADS].set(gdn_dt_bias[0])
    prm = prm.at[2, 2 * GDN_HEADS:2 * GDN_HEADS + FOX_HEADS].set(fox_f_bias[0])

    hn = rmsnorm(x, norm_g[0, 0], BF16)
    proj_a = matmul(hn, w_a, F32, tm=tm, tn=512, name="proj_gdn")
    proj_b = matmul(hn, w_b, BF16, tm=tm, tn=512, name="proj_fox")
    gates_col = gates(hn, w_g, prm, rows=min(512, t))
    gates_row = gates_col.T.reshape(GATE_LANES, 1, t)
    o_a = gdn_mixer(proj_a, gdn_conv_w[0], gates_col, gates_row, gdn_o_norm[0])
    o_b = fox_attention(proj_b, gates_col, gates_row, tq=min(512, t), tk=min(512, t))
    o = jnp.concatenate([o_a, o_b], axis=1)
    m = matmul(o, w_out[0].astype(BF16), BF16, tm=tm, tn=512, name="out_proj")
    x, hn = resid_norm(x, m, norm_g[0, 1], norm_g[0, 2], BF16)
    f = _ffn(hn, ffn_w_gate[0], ffn_w_up[0], ffn_w_down[0])
    x, hn = resid_norm(x, f, norm_g[0, 3], norm_g[1, 0], F32)

    y = s5_mixer(hn, s5_A_re[0], s5_A_im[0], s5_log_step[0], s5_B_re[0], s5_B_im[0],
                 s5_C_re[0], s5_C_im[0], s5_D[0])
    m = matmul_glu(y, s5_w_glu_a[0].astype(BF16), s5_w_glu_b[0].astype(BF16), tm=tm, tn=256)
    x, hn = resid_norm(x, m, norm_g[1, 1], norm_g[1, 2], BF16)
    f = _ffn(hn, ffn_w_gate[1], ffn_w_up[1], ffn_w_down[1])
    x = resid_norm(x, f, norm_g[1, 3], None, None)
    return x.reshape(bsz, t, d)
```

```python
import functools
import math

import jax
import jax.numpy as jnp
from jax import lax
from jax.experimental import pallas as pl
from jax.experimental.pallas import tpu as pltpu

F32 = jnp.float32
BF16 = jnp.bfloat16

LANES = 128
SUBLANES = 8
VMEM_LIMIT_BYTES = 56 * 1024 * 1024

HEAD_DIM = 128
GDN_HEADS = 16
FOX_HEADS = 16
CONV_WIDTH = 4
GDN_CHUNK = 128
S5_GROUP = 16
S5_STATE = 64
S5_CHUNK = 16
S5_MAX_RE = -1e-4
RMS_EPS = 1e-6
L2_EPS = 1e-6
GATE_LANES = LANES
NEG_INF = float("-inf")
LOG2E = math.log2(math.e)


def _params(*semantics):
    return pltpu.CompilerParams(dimension_semantics=semantics,
                                vmem_limit_bytes=VMEM_LIMIT_BYTES)


def _silu(x):
    return x * jax.nn.sigmoid(x)


def _softplus(x):
    return jnp.maximum(x, 0.0) + jnp.log1p(jnp.exp(-jnp.abs(x)))


def _dot(a, b):
    return jnp.dot(a, b, preferred_element_type=F32)


def _dot_nt(a, b):
    return lax.dot_general(a, b, (((1,), (1,)), ((), ())), preferred_element_type=F32)


def _dot_nt_f32(a, b):
    return lax.dot_general(a, b, (((1,), (1,)), ((), ())), precision=lax.Precision.HIGHEST,
                           preferred_element_type=F32)


def _dot_tn(a, b):
    return lax.dot_general(a, b, (((0,), (0,)), ((), ())), preferred_element_type=F32)


def _split2(x):
    hi = x.astype(BF16)
    lo = (x - hi.astype(F32)).astype(BF16)
    return hi, lo


def _rms(x, g):
    y = x * lax.rsqrt(jnp.mean(x * x, axis=-1, keepdims=True) + RMS_EPS)
    return y * g


def _rmsnorm_kernel(x_ref, g_ref, o_ref):
    o_ref[...] = _rms(x_ref[...], g_ref[...]).astype(o_ref.dtype)


def rmsnorm(x, g, out_dtype, rows=256):
    t, d = x.shape
    return pl.pallas_call(
        _rmsnorm_kernel,
        grid=(t // rows,),
        in_specs=[pl.BlockSpec((rows, d), lambda i: (i, 0)),
                  pl.BlockSpec((1, d), lambda i: (0, 0))],
        out_specs=pl.BlockSpec((rows, d), lambda i: (i, 0)),
        out_shape=jax.ShapeDtypeStruct((t, d), out_dtype),
        compiler_params=_params("parallel"),
        name="rmsnorm",
    )(x, g.reshape(1, d))


def _resid_norm_kernel(x_ref, m_ref, gp_ref, gn_ref, xo_ref, ho_ref):
    xn = x_ref[...] + _rms(m_ref[...].astype(F32), gp_ref[...])
    xo_ref[...] = xn
    ho_ref[...] = _rms(xn, gn_ref[...]).astype(ho_ref.dtype)


def _resid_kernel(x_ref, m_ref, gp_ref, xo_ref):
    xo_ref[...] = x_ref[...] + _rms(m_ref[...].astype(F32), gp_ref[...])


def resid_norm(x, m, g_post, g_next, next_dtype, rows=256):
    t, d = x.shape
    row_spec = pl.BlockSpec((rows, d), lambda i: (i, 0))
    vec_spec = pl.BlockSpec((1, d), lambda i: (0, 0))
    if g_next is None:
        return pl.pallas_call(
            _resid_kernel, grid=(t // rows,),
            in_specs=[row_spec, row_spec, vec_spec], out_specs=row_spec,
            out_shape=jax.ShapeDtypeStruct((t, d), F32),
            compiler_params=_params("parallel"), name="resid",
        )(x, m, g_post.reshape(1, d))
    return pl.pallas_call(
        _resid_norm_kernel, grid=(t // rows,),
        in_specs=[row_spec, row_spec, vec_spec, vec_spec],
        out_specs=[row_spec, row_spec],
        out_shape=[jax.ShapeDtypeStruct((t, d), F32), jax.ShapeDtypeStruct((t, d), next_dtype)],
        compiler_params=_params("parallel"), name="resid_norm",
    )(x, m, g_post.reshape(1, d), g_next.reshape(1, d))


def _mm_kernel(a_ref, w_ref, o_ref):
    o_ref[...] = _dot(a_ref[...], w_ref[...]).astype(o_ref.dtype)


def _mm_swiglu_kernel(a_ref, wg_ref, wu_ref, o_ref):
    a = a_ref[...]
    o_ref[...] = (_silu(_dot(a, wg_ref[...])) * _dot(a, wu_ref[...])).astype(o_ref.dtype)


def _mm_glu_kernel(a_ref, wa_ref, wb_ref, o_ref):
    a = a_ref[...]
    o_ref[...] = (_dot(a, wa_ref[...]) * jax.nn.sigmoid(_dot(a, wb_ref[...]))).astype(o_ref.dtype)


def _matmul_call(body, a, weights, out_dtype, tm, tn, name):
    m, k = a.shape
    n = weights[0].shape[1]
    w_spec = pl.BlockSpec((k, tn), lambda i, j: (0, j))
    return pl.pallas_call(
        body,
        grid=(m // tm, n // tn),
        in_specs=[pl.BlockSpec((tm, k), lambda i, j: (i, 0))] + [w_spec] * len(weights),
        out_specs=pl.BlockSpec((tm, tn), lambda i, j: (i, j)),
        out_shape=jax.ShapeDtypeStruct((m, n), out_dtype),
        compiler_params=_params("parallel", "arbitrary"),
        name=name,
    )(a, *weights)


def matmul(a, w, out_dtype, tm, tn, name="matmul", scale=None):
    body = _mm_kernel if scale is None else functools.partial(_mm_scaled_kernel, scale=scale)
    return _matmul_call(body, a, (w,), out_dtype, tm, tn, name)


def _mm_scaled_kernel(a_ref, w_ref, o_ref, *, scale):
    o_ref[...] = (_dot(a_ref[...], w_ref[...]) * scale).astype(o_ref.dtype)


def _mm_nt_kernel(a_ref, wt_ref, o_ref):
    o_ref[...] = _dot_nt(wt_ref[...], a_ref[...]).astype(o_ref.dtype)


def matmul_nt(a, wt, out_dtype, tm, tn, name):
    m, k = a.shape
    n = wt.shape[0]
    return pl.pallas_call(
        _mm_nt_kernel,
        grid=(m // tm, n // tn),
        in_specs=[pl.BlockSpec((tm, k), lambda i, j: (i, 0)),
                  pl.BlockSpec((tn, k), lambda i, j: (j, 0))],
        out_specs=pl.BlockSpec((tn, tm), lambda i, j: (j, i)),
        out_shape=jax.ShapeDtypeStruct((n, m), out_dtype),
        compiler_params=_params("parallel", "arbitrary"),
        name=name,
    )(a, wt)


def _mm2_kernel(a1_ref, a2_ref, w1_ref, w2_ref, o_ref):
    o_ref[...] = (_dot(a1_ref[...], w1_ref[...]) + _dot(a2_ref[...], w2_ref[...])).astype(o_ref.dtype)


def matmul2(a1, a2, w1, w2, out_dtype, tm, tn, name):
    m, k1 = a1.shape
    k2 = a2.shape[1]
    n = w1.shape[1]
    return pl.pallas_call(
        _mm2_kernel,
        grid=(m // tm, n // tn),
        in_specs=[pl.BlockSpec((tm, k1), lambda i, j: (i, 0)),
                  pl.BlockSpec((tm, k2), lambda i, j: (i, 0)),
                  pl.BlockSpec((k1, tn), lambda i, j: (0, j)),
                  pl.BlockSpec((k2, tn), lambda i, j: (0, j))],
        out_specs=pl.BlockSpec((tm, tn), lambda i, j: (i, j)),
        out_shape=jax.ShapeDtypeStruct((m, n), out_dtype),
        compiler_params=_params("parallel", "arbitrary"),
        name=name,
    )(a1, a2, w1, w2)


def matmul_swiglu(a, wg, wu, tm, tn):
    return _matmul_call(_mm_swiglu_kernel, a, (wg, wu), BF16, tm, tn, "ffn_gate_up")


def matmul_glu(a, wa, wb, tm, tn):
    return _matmul_call(_mm_glu_kernel, a, (wa, wb), BF16, tm, tn, "s5_glu")


def _gates_kernel(hn_ref, w_ref, prm_ref, o_ref, carry_ref, *, rows):
    @pl.when(pl.program_id(0) == 0)
    def _():
        carry_ref[...] = jnp.zeros_like(carry_ref)

    p = _dot(hn_ref[...], w_ref[...])
    lane = lax.broadcasted_iota(jnp.int32, (1, GATE_LANES), 1)
    is_decay = lane < GDN_HEADS
    is_beta = (lane >= GDN_HEADS) & (lane < 2 * GDN_HEADS)
    is_forget = (lane >= 2 * GDN_HEADS) & (lane < 2 * GDN_HEADS + FOX_HEADS)
    a_log, dt_bias, f_bias = prm_ref[0:1, :], prm_ref[1:2, :], prm_ref[2:3, :]
    decay = -jnp.exp(a_log) * _softplus(p + dt_bias)
    beta = jax.nn.sigmoid(p)
    log_f = -_softplus(-(p + f_bias))
    vals = jnp.where(is_decay, decay, jnp.where(is_forget, log_f, 0.0))

    r = lax.broadcasted_iota(jnp.int32, (GDN_CHUNK, GDN_CHUNK), 0)
    c = lax.broadcasted_iota(jnp.int32, (GDN_CHUNK, GDN_CHUNK), 1)
    tri = (r >= c).astype(BF16)
    carry = carry_ref[...]
    for ci in range(rows // GDN_CHUNK):
        sl = slice(ci * GDN_CHUNK, (ci + 1) * GDN_CHUNK)
        v = vals[sl]
        hi = v.astype(BF16)
        r1 = v - hi.astype(F32)
        mid = r1.astype(BF16)
        lo = (r1 - mid.astype(F32)).astype(BF16)
        cs = _dot(tri, hi) + _dot(tri, mid) + _dot(tri, lo)
        run = cs + carry
        o_ref[sl, :] = jnp.where(is_decay, cs, jnp.where(is_beta, beta[sl], jnp.where(is_forget, run, 0.0)))
        carry = jnp.where(is_forget, run[GDN_CHUNK - 1:GDN_CHUNK, :], 0.0)
    carry_ref[...] = carry


def gates(hn, w_gates, prm, rows=512):
    t, d = hn.shape
    return pl.pallas_call(
        functools.partial(_gates_kernel, rows=rows),
        grid=(t // rows,),
        in_specs=[pl.BlockSpec((rows, d), lambda i: (i, 0)),
                  pl.BlockSpec((d, GATE_LANES), lambda i: (0, 0)),
                  pl.BlockSpec((SUBLANES, GATE_LANES), lambda i: (0, 0))],
        out_specs=pl.BlockSpec((rows, GATE_LANES), lambda i: (i, 0)),
        out_shape=jax.ShapeDtypeStruct((t, GATE_LANES), F32),
        scratch_shapes=[pltpu.VMEM((1, GATE_LANES), F32)],
        compiler_params=_params("arbitrary"),
        name="gates",
    )(hn, w_gates, prm)


def _dot_split_parts(ah, al, bh, bl):
    return _dot(jnp.concatenate([ah, al], axis=1), jnp.concatenate([bh, bh], axis=0)) + _dot(ah, bl)


def _unit_lower_inverses(lmats):
    n = lmats[0].shape[0]
    r = lax.broadcasted_iota(jnp.int32, (n, n), 0)
    c = lax.broadcasted_iota(jnp.int32, (n, n), 1)
    zero = jnp.zeros((), BF16)
    lsplit = [_split2(l) for l in lmats]
    invs = [(r == c).astype(F32) for _ in lmats]
    shift = 0
    while (1 << shift) < n:
        rb = r >> shift
        joins = ((rb & 1) == 1) & ((c >> shift) == rb - 1)
        isplit = [_split2(inv) for inv in invs]
        prods = [_dot_split_parts(jnp.where(joins, lh, zero), jnp.where(joins, ll, zero), ih, il)
                 for (lh, ll), (ih, il) in zip(lsplit, isplit)]
        invs = [inv - _dot_split_parts(ih, il, *_split2(p))
                for inv, (ih, il), p in zip(invs, isplit, prods)]
        shift += 1
    return invs


def _gdn_kernel(q_ref, k_ref, v_ref, z_ref, qh_ref, kh_ref, vh_ref, cwq_ref, cwk_ref, cwv_ref,
                gcol_ref, grow_ref, onorm_ref, o_ref, s_ref, *, rows):
    h = pl.program_id(0)
    first = pl.program_id(1) == 0

    @pl.when(first)
    def _():
        s_ref[...] = jnp.zeros_like(s_ref)

    def conv_silu(x_ref, halo_ref, w_ref):
        x = x_ref[...]
        halo = jnp.where(first, 0.0, halo_ref[...])
        xe = jnp.concatenate([halo, x], axis=0)
        w = w_ref[...]
        y = w[CONV_WIDTH - 1:CONV_WIDTH, :] * x
        for i in range(CONV_WIDTH - 1):
            off = SUBLANES - (CONV_WIDTH - 1) + i
            y = y + w[i:i + 1, :] * xe[off:off + rows, :]
        return _silu(y)

    def l2norm(x):
        return x * lax.rsqrt(jnp.sum(x * x, axis=-1, keepdims=True) + L2_EPS)

    q = l2norm(conv_silu(q_ref, qh_ref, cwq_ref)) * (HEAD_DIM ** -0.5)
    k = l2norm(conv_silu(k_ref, kh_ref, cwk_ref))
    v = conv_silu(v_ref, vh_ref, cwv_ref)

    lane = lax.broadcasted_iota(jnp.int32, (1, GATE_LANES), 1)
    gates_col = gcol_ref[...]
    g_col = jnp.sum(jnp.where(lane == h, gates_col, 0.0), axis=1, keepdims=True)
    beta = jnp.sum(jnp.where(lane == h + GDN_HEADS, gates_col, 0.0), axis=1, keepdims=True)
    g_row = grow_ref[0]

    n = GDN_CHUNK
    r = lax.broadcasted_iota(jnp.int32, (n, n), 0)
    c = lax.broadcasted_iota(jnp.int32, (n, n), 1)
    chunks = [slice(ci * n, (ci + 1) * n) for ci in range(rows // n)]

    local = []
    for sl in chunks:
        qc, kc, vc, gc, bc = q[sl], k[sl], v[sl], g_col[sl], beta[sl]
        decay = jnp.exp(jnp.where(r >= c, gc - g_row[:, sl], NEG_INF))
        kb = kc * bc
        k16 = kc.astype(BF16)
        qk = _dot_nt(jnp.concatenate([kb, qc], axis=0).astype(BF16), k16) * jnp.tile(decay, (2, 1))
        eg = jnp.exp(gc)
        g_last = gc[n - 1:n, :]
        local.append(dict(
            lmat=jnp.where(r > c, qk[:n], 0.0), attn=qk[n:].astype(BF16),
            rhs=jnp.concatenate([vc * bc, kb * eg], axis=1), q_dec=(qc * eg).astype(BF16),
            k_dec=(kc * jnp.exp(g_last - gc)).astype(BF16), s_dec=jnp.exp(g_last)))
    invs = _unit_lower_inverses([lc["lmat"] for lc in local])
    for lc, inv in zip(local, invs):
        sol = _dot_split_parts(*_split2(inv), *_split2(lc["rhs"]))
        lc["u"], lc["w"] = sol[:, :HEAD_DIM], sol[:, HEAD_DIM:].astype(BF16)

    state = s_ref[...]
    for sl, lc in zip(chunks, local):
        s16 = state.astype(BF16)
        v_new = lc["u"] - _dot(lc["w"], s16)
        vn16 = v_new.astype(BF16)
        o = _dot(jnp.concatenate([lc["q_dec"], lc["attn"]], axis=1), jnp.concatenate([s16, vn16], axis=0))
        state = state * lc["s_dec"] + _dot_tn(lc["k_dec"], vn16)
        o_ref[sl, :] = (_rms(o, onorm_ref[...]) * _silu(z_ref[sl, :])).astype(o_ref.dtype)
    s_ref[...] = state


def gdn_mixer(proj_a, conv_w, gates_col, gates_row, o_norm, rows=512):
    t = proj_a.shape[0]
    nh = GDN_HEADS
    halo_blocks = rows // SUBLANES

    def col(section):
        return pl.BlockSpec((rows, HEAD_DIM), lambda h, i: (i, section * nh + h))

    def halo(section):
        return pl.BlockSpec((SUBLANES, HEAD_DIM),
                            lambda h, i: (jnp.maximum(i * halo_blocks - 1, 0), section * nh + h))

    def cw(section):
        return pl.BlockSpec((CONV_WIDTH, HEAD_DIM), lambda h, i: (0, section * nh + h))

    return pl.pallas_call(
        functools.partial(_gdn_kernel, rows=rows),
        grid=(nh, t // rows),
        in_specs=[col(0), col(1), col(2), col(3), halo(0), halo(1), halo(2), cw(0), cw(1), cw(2),
                  pl.BlockSpec((rows, GATE_LANES), lambda h, i: (i, 0)),
                  pl.BlockSpec((1, 1, rows), lambda h, i: (h, 0, i)),
                  pl.BlockSpec((1, HEAD_DIM), lambda h, i: (0, 0))],
        out_specs=pl.BlockSpec((rows, HEAD_DIM), lambda h, i: (i, h)),
        out_shape=jax.ShapeDtypeStruct((t, nh * HEAD_DIM), BF16),
        scratch_shapes=[pltpu.VMEM((HEAD_DIM, HEAD_DIM), F32)],
        compiler_params=_params("parallel", "arbitrary"),
        name="gdn",
    )(proj_a, proj_a, proj_a, proj_a, proj_a, proj_a, proj_a, conv_w, conv_w, conv_w,
      gates_col, gates_row, o_norm.reshape(1, HEAD_DIM))


FOX_Q_SCALE = LOG2E * HEAD_DIM ** -0.5


def _fox_kernel(q_ref, kt_ref, v_ref, grow_ref, o_ref, m_ref, l_ref, acc_ref, *, tq, wide):
    qi = pl.program_id(1)
    q0 = pl.multiple_of(qi * tq, tq)
    m_ref[...] = jnp.full_like(m_ref, NEG_INF)
    l_ref[...] = jnp.zeros_like(l_ref)
    acc_ref[...] = jnp.zeros_like(acc_ref)
    cum0 = grow_ref[0, :, pl.ds(q0, LANES)][:, 0:1]
    q = q_ref[...]

    def block(k0, width, diagonal):
        kt = kt_ref[:, pl.ds(k0, width)]
        vj = v_ref[pl.ds(k0, width), :]
        bias = (cum0 - grow_ref[0, :, pl.ds(k0, width)]) * LOG2E
        s = _dot(q, kt) + bias
        if diagonal:
            r = lax.broadcasted_iota(jnp.int32, (tq, width), 0)
            c = lax.broadcasted_iota(jnp.int32, (tq, width), 1)
            s = jnp.where(c <= r, s, NEG_INF)
        m_prev = m_ref[...]
        m_new = jnp.maximum(m_prev, jnp.max(s, axis=1, keepdims=True))
        alpha = jnp.exp2(m_prev - m_new)
        p = jnp.exp2(s - jnp.tile(m_new, (1, width // LANES)))
        p_sum = p[:, :LANES]
        for ct in range(1, width // LANES):
            p_sum = p_sum + p[:, ct * LANES:(ct + 1) * LANES]
        l_ref[...] = alpha * l_ref[...] + p_sum
        acc_ref[...] = alpha * acc_ref[...] + _dot(p.astype(BF16), vj)
        m_ref[...] = m_new

    per_wide = wide // tq
    n_wide = qi // per_wide

    def wide_block(j, carry):
        block(pl.multiple_of(j * wide, wide), wide, False)
        return carry

    lax.fori_loop(0, n_wide, wide_block, 0)
    for rem in range(per_wide - 1):
        @pl.when(qi - n_wide * per_wide > rem)
        def _():
            block(pl.multiple_of((n_wide * per_wide + rem) * tq, tq), tq, False)
    block(q0, tq, True)
    l = jnp.sum(l_ref[...], axis=1, keepdims=True)
    o_ref[...] = (acc_ref[...] / l).astype(o_ref.dtype)


def fox_attention(q, kt, v, gates_row, tq=512, wide=2048):
    t = q.shape[0]
    nh = FOX_HEADS
    wide = max(tq, min(wide, t))
    return pl.pallas_call(
        functools.partial(_fox_kernel, tq=tq, wide=wide),
        grid=(nh, t // tq),
        in_specs=[pl.BlockSpec((tq, HEAD_DIM), lambda h, qi: (qi, h)),
                  pl.BlockSpec((HEAD_DIM, t), lambda h, qi: (h, 0)),
                  pl.BlockSpec((t, HEAD_DIM), lambda h, qi: (0, h)),
                  pl.BlockSpec((1, 1, t), lambda h, qi: (2 * GDN_HEADS + h, 0, 0))],
        out_specs=pl.BlockSpec((tq, HEAD_DIM), lambda h, qi: (qi, h)),
        out_shape=jax.ShapeDtypeStruct((t, nh * HEAD_DIM), BF16),
        scratch_shapes=[pltpu.VMEM((tq, LANES), F32), pltpu.VMEM((tq, LANES), F32),
                        pltpu.VMEM((tq, HEAD_DIM), F32)],
        compiler_params=_params("parallel", "arbitrary"),
        name="fox",
    )(q, kt, v, gates_row)


S5_SCAN_LEVELS = 16


def _s5_setup_kernel(are_ref, aim_ref, ls_ref, bre_ref, bim_ref, cre_ref, cim_ref,
                     g_ref, ere_ref, eim_ref, pre_ref, pim_ref, sre_ref, sim_ref, *, gb):
    p, l = S5_GROUP, S5_CHUNK
    lam_re = jnp.minimum(are_ref[...], S5_MAX_RE)
    lam_im = aim_ref[...]
    dt = jnp.exp(ls_ref[...])
    mag = jnp.exp(lam_re * dt)
    a_re = mag * jnp.cos(lam_im * dt)
    a_im = mag * jnp.sin(lam_im * dt)
    nr, ni = a_re - 1.0, a_im
    den = lam_re * lam_re + lam_im * lam_im
    f_re = (nr * lam_re + ni * lam_im) / den
    f_im = (ni * lam_re - nr * lam_im) / den
    b_re, b_im = bre_ref[...], bim_ref[...]
    bb_re = f_re * b_re - f_im * b_im
    bb_im = f_re * b_im + f_im * b_re
    c_re, c_im = cre_ref[...], cim_ref[...]

    pows = [(jnp.ones_like(a_re), jnp.zeros_like(a_im))]
    for _ in range(l):
        pr, pi_ = pows[-1]
        pows.append((pr * a_re - pi_ * a_im, pr * a_im + pi_ * a_re))

    ca_re = [c_re * pr - c_im * pi_ for pr, pi_ in pows]
    ca_im = [c_re * pi_ + c_im * pr for pr, pi_ in pows]
    lo_re = jnp.concatenate(ca_re[:l], axis=1)
    lo_im = jnp.concatenate(ca_im[:l], axis=1)
    for gi in range(gb):
        g_ref[gi] = _dot_nt_f32(lo_re[gi], bb_re[gi]) - _dot_nt_f32(lo_im[gi], bb_im[gi])
    pre_ref[...] = jnp.concatenate(ca_re[1:], axis=1)
    pim_ref[...] = -jnp.concatenate(ca_im[1:], axis=1)
    ere_ref[...] = jnp.concatenate(
        [pows[l - 1 - j][0] * bb_re - pows[l - 1 - j][1] * bb_im for j in range(l)], axis=1)
    eim_ref[...] = jnp.concatenate(
        [pows[l - 1 - j][0] * bb_im + pows[l - 1 - j][1] * bb_re for j in range(l)], axis=1)
    sr, si = pows[l]
    for lv in range(S5_SCAN_LEVELS):
        sre_ref[:, lv:lv + 1, :] = sr
        sim_ref[:, lv:lv + 1, :] = si
        sr, si = sr * sr - si * si, 2.0 * sr * si


def s5_setup(a_re, a_im, log_step, b_re_t, b_im_t, c_re, c_im, gb=8):
    g, n = a_re.shape
    p, l = S5_GROUP, S5_CHUNK
    vec = pl.BlockSpec((gb, 1, n), lambda i: (i, 0, 0))
    mat = pl.BlockSpec((gb, p, n), lambda i: (i, 0, 0))
    big = pl.BlockSpec((gb, l * p, n), lambda i: (i, 0, 0))
    lvl = pl.BlockSpec((gb, S5_SCAN_LEVELS, n), lambda i: (i, 0, 0))
    big_shape = jax.ShapeDtypeStruct((g, l * p, n), F32)
    lvl_shape = jax.ShapeDtypeStruct((g, S5_SCAN_LEVELS, n), F32)
    return pl.pallas_call(
        functools.partial(_s5_setup_kernel, gb=gb),
        grid=(g // gb,),
        in_specs=[vec, vec, pl.BlockSpec((gb, 1, 1), lambda i: (i, 0, 0)), mat, mat, mat, mat],
        out_specs=[pl.BlockSpec((gb, l * p, p), lambda i: (i, 0, 0)), big, big, big, big, lvl, lvl],
        out_shape=[jax.ShapeDtypeStruct((g, l * p, p), F32), big_shape, big_shape, big_shape, big_shape,
                   lvl_shape, lvl_shape],
        compiler_params=_params("parallel"),
        name="s5_setup",
    )(a_re.reshape(g, 1, n), a_im.reshape(g, 1, n), log_step.reshape(g, 1, 1), b_re_t, b_im_t, c_re, c_im)


def _s5_kernel(u_ref, kt_ref, e_ref, p_ref, s1_ref, s2_ref, d_ref, o_ref, *, gb, levels):
    nc = u_ref.shape[1]
    half = S5_STATE
    row = lax.broadcasted_iota(jnp.int32, (nc, 2 * half), 0)
    for gi in range(gb):
        u = u_ref[gi]
        u16 = u.astype(BF16)
        x = _dot(u16, e_ref[gi])
        for lv in range(levels):
            sh = 1 << lv
            xs = jnp.where(row >= sh, pltpu.roll(x, sh, axis=0), 0.0)
            xs_sw = pltpu.roll(xs, half, axis=1)
            x = x + s1_ref[gi, lv:lv + 1, :] * xs + s2_ref[gi, lv:lv + 1, :] * xs_sw
        s_in = jnp.where(row >= 1, pltpu.roll(x, 1, axis=0), 0.0)
        y = _dot(u16, kt_ref[gi]) + _dot_nt(s_in.astype(BF16), p_ref[gi]) + d_ref[gi] * u
        o_ref[gi] = jax.nn.gelu(y).astype(o_ref.dtype)


def s5_scan(u_r, kt, emat, pmat, s1, s2, d_t, gb=4):
    g, nc, w = u_r.shape
    levels = max(1, math.ceil(math.log2(nc)))
    assert levels <= S5_SCAN_LEVELS

    def blk(*shape):
        return pl.BlockSpec((gb,) + shape, lambda i: (i,) + (0,) * len(shape))

    return pl.pallas_call(
        functools.partial(_s5_kernel, gb=gb, levels=levels),
        grid=(g // gb,),
        in_specs=[blk(nc, w), blk(w, w), blk(w, 2 * S5_STATE), blk(w, 2 * S5_STATE),
                  blk(S5_SCAN_LEVELS, 2 * S5_STATE), blk(S5_SCAN_LEVELS, 2 * S5_STATE), blk(1, w)],
        out_specs=blk(nc, w),
        out_shape=jax.ShapeDtypeStruct((g, nc, w), BF16),
        compiler_params=_params("parallel"),
        name="s5_scan",
    )(u_r, kt, emat, pmat, s1, s2, d_t)


def s5_mixer(hn, a_re, a_im, log_step, b_re, b_im, c_re, c_im, d_skip):
    t, d = hn.shape
    g, n = a_re.shape
    p, l = S5_GROUP, S5_CHUNK
    nc = t // l
    gm, e_re, e_im, p_re, p_im, s_re, s_im = s5_setup(
        a_re, a_im, log_step, b_re.transpose(0, 2, 1), b_im.transpose(0, 2, 1), c_re, c_im)
    gm4 = gm.reshape(g, l, p, p)
    jj = jnp.arange(l)[:, None]
    tt = jnp.arange(l)[None, :]
    blocks = jnp.where((tt >= jj)[None, :, :, None, None], gm4[:, jnp.clip(tt - jj, 0, l - 1)], 0.0)
    kt = blocks.transpose(0, 1, 4, 2, 3).reshape(g, l * p, l * p).astype(BF16)
    emat = jnp.concatenate([e_re, e_im], axis=-1).astype(BF16)
    pmat = jnp.concatenate([p_re, p_im], axis=-1).astype(BF16)
    s1 = jnp.concatenate([s_re, s_re], axis=-1)
    s2 = jnp.concatenate([-s_im, s_im], axis=-1)
    d_t = jnp.tile(d_skip.reshape(g, 1, p), (1, 1, l))
    u_r = hn.reshape(nc, l, g, p).transpose(2, 0, 1, 3).reshape(g, nc, l * p)
    y_r = s5_scan(u_r, kt, emat, pmat, s1, s2, d_t)
    return y_r.reshape(g, nc, l, p).transpose(1, 2, 0, 3).reshape(t, d)


def _ffn(hn, w_gate, w_up, w_down):
    hidden = matmul_swiglu(hn, w_gate.astype(BF16), w_up.astype(BF16), tm=1024, tn=256)
    return matmul(hidden, w_down.astype(BF16), BF16, tm=512, tn=256, name="ffn_down")


def kernel(x, norm_g, w_in, gdn_conv_w, gdn_A_log, gdn_dt_bias, gdn_o_norm, fox_f_bias, w_out, s5_A_re, s5_A_im, s5_log_step, s5_B_re, s5_B_im, s5_C_re, s5_C_im, s5_D, s5_w_glu_a, s5_w_glu_b, ffn_w_gate, ffn_w_up, ffn_w_down):
    bsz, t, d = x.shape
    assert bsz == 1
    x = x.reshape(t, d)
    gw = GDN_HEADS * HEAD_DIM
    fw = FOX_HEADS * HEAD_DIM
    tm = min(1024, t)

    w0 = w_in[0]
    o_gate = 4 * gw
    o_fox = o_gate + 2 * GDN_HEADS
    o_forget = o_fox + 3 * fw
    w_a = w0[:, :o_gate].astype(BF16)
    w_fq = w0[:, o_fox:o_fox + fw].astype(BF16)
    w_fkt = w0[:, o_fox + fw:o_fox + 2 * fw].T.astype(BF16)
    w_fv = w0[:, o_fox + 2 * fw:o_forget].astype(BF16)
    pad = GATE_LANES - 2 * GDN_HEADS - FOX_HEADS
    w_g = jnp.concatenate([w0[:, o_gate:o_fox], w0[:, o_forget:], jnp.zeros((d, pad), F32)], axis=1).astype(BF16)
    prm = jnp.zeros((SUBLANES, GATE_LANES), F32)
    prm = prm.at[0, :GDN_HEADS].set(gdn_A_log[0]).at[1, :GDN_HEADS].set(gdn_dt_bias[0])
    prm = prm.at[2, 2 * GDN_HEADS:2 * GDN_HEADS + FOX_HEADS].set(fox_f_bias[0])

    hn = rmsnorm(x, norm_g[0, 0], BF16)
    proj_a = matmul(hn, w_a, F32, tm=tm, tn=512, name="proj_gdn")
    fox_q = matmul(hn, w_fq, BF16, tm=tm, tn=512, name="proj_fox_q", scale=FOX_Q_SCALE)
    fox_kt = matmul_nt(hn, w_fkt, BF16, tm=tm, tn=512, name="proj_fox_kt")
    fox_v = matmul(hn, w_fv, BF16, tm=tm, tn=512, name="proj_fox_v")
    gates_col = gates(hn, w_g, prm, rows=min(512, t))
    gates_row = gates_col.T.reshape(GATE_LANES, 1, t)
    o_a = gdn_mixer(proj_a, gdn_conv_w[0], gates_col, gates_row, gdn_o_norm[0])
    o_b = fox_attention(fox_q, fox_kt, fox_v, gates_row, tq=min(512, t))
    m = matmul2(o_a, o_b, w_out[0, :gw].astype(BF16), w_out[0, gw:].astype(BF16), BF16,
                tm=tm, tn=512, name="out_proj")
    x, hn = resid_norm(x, m, norm_g[0, 1], norm_g[0, 2], BF16)
    f = _ffn(hn, ffn_w_gate[0], ffn_w_up[0], ffn_w_down[0])
    x, hn = resid_norm(x, f, norm_g[0, 3], norm_g[1, 0], F32)

    y = s5_mixer(hn, s5_A_re[0], s5_A_im[0], s5_log_step[0], s5_B_re[0], s5_B_im[0],
                 s5_C_re[0], s5_C_im[0], s5_D[0])
    m = matmul_glu(y, s5_w_glu_a[0].astype(BF16), s5_w_glu_b[0].astype(BF16), tm=tm, tn=256)
    x, hn = resid_norm(x, m, norm_g[1, 1], norm_g[1, 2], BF16)
    f = _ffn(hn, ffn_w_gate[1], ffn_w_up[1], ffn_w_down[1])
    x = resid_norm(x, f, norm_g[1, 3], None, None)
    return x.reshape(bsz, t, d)
```

```python
import functools
import math
from typing import NamedTuple

import jax
import jax.numpy as jnp
from jax import lax
from jax.experimental import pallas as pl
from jax.experimental.pallas import tpu as pltpu

F32 = jnp.float32
BF16 = jnp.bfloat16

LANES = 128
SUBLANES = 8
VMEM_LIMIT_BYTES = 56 * 1024 * 1024

HEAD_DIM = 128
GDN_HEADS = 16
FOX_HEADS = 16
CONV_WIDTH = 4
GDN_CHUNK = 128
S5_GROUP = 16
S5_STATE = 64
S5_CHUNK = 16
S5_MAX_RE = -1e-4
RMS_EPS = 1e-6
L2_EPS = 1e-6
GATE_LANES = LANES
NEG_INF = float("-inf")
LOG2E = math.log2(math.e)


def _params(*semantics):
    return pltpu.CompilerParams(dimension_semantics=semantics,
                                vmem_limit_bytes=VMEM_LIMIT_BYTES)


def _silu(x):
    return x * jax.nn.sigmoid(x)


def _softplus(x):
    return jnp.maximum(x, 0.0) + jnp.log1p(jnp.exp(-jnp.abs(x)))


def _dot(a, b):
    return jnp.dot(a, b, preferred_element_type=F32)


def _dot_nt(a, b):
    return lax.dot_general(a, b, (((1,), (1,)), ((), ())), preferred_element_type=F32)


def _dot_nt_f32(a, b):
    return lax.dot_general(a, b, (((1,), (1,)), ((), ())), precision=lax.Precision.HIGHEST,
                           preferred_element_type=F32)


def _dot_tn(a, b):
    return lax.dot_general(a, b, (((0,), (0,)), ((), ())), preferred_element_type=F32)


def _split2(x):
    hi = x.astype(BF16)
    lo = (x - hi.astype(F32)).astype(BF16)
    return hi, lo


def _rms(x, g):
    y = x * lax.rsqrt(jnp.mean(x * x, axis=-1, keepdims=True) + RMS_EPS)
    return y * g


def _rmsnorm_kernel(x_ref, g_ref, o_ref):
    o_ref[...] = _rms(x_ref[...], g_ref[...]).astype(o_ref.dtype)


def rmsnorm(x, g, out_dtype, rows=256):
    t, d = x.shape
    return pl.pallas_call(
        _rmsnorm_kernel,
        grid=(t // rows,),
        in_specs=[pl.BlockSpec((rows, d), lambda i: (i, 0)),
                  pl.BlockSpec((1, d), lambda i: (0, 0))],
        out_specs=pl.BlockSpec((rows, d), lambda i: (i, 0)),
        out_shape=jax.ShapeDtypeStruct((t, d), out_dtype),
        compiler_params=_params("parallel"),
        name="rmsnorm",
    )(x, g.reshape(1, d))


def _resid_norm_kernel(x_ref, m_ref, gp_ref, gn_ref, xo_ref, ho_ref):
    xn = x_ref[...] + _rms(m_ref[...].astype(F32), gp_ref[...])
    xo_ref[...] = xn
    ho_ref[...] = _rms(xn, gn_ref[...]).astype(ho_ref.dtype)


def _resid_kernel(x_ref, m_ref, gp_ref, xo_ref):
    xo_ref[...] = x_ref[...] + _rms(m_ref[...].astype(F32), gp_ref[...])


def resid_norm(x, m, g_post, g_next, next_dtype, rows=256):
    t, d = x.shape
    row_spec = pl.BlockSpec((rows, d), lambda i: (i, 0))
    vec_spec = pl.BlockSpec((1, d), lambda i: (0, 0))
    if g_next is None:
        return pl.pallas_call(
            _resid_kernel, grid=(t // rows,),
            in_specs=[row_spec, row_spec, vec_spec], out_specs=row_spec,
            out_shape=jax.ShapeDtypeStruct((t, d), F32),
            compiler_params=_params("parallel"), name="resid",
        )(x, m, g_post.reshape(1, d))
    return pl.pallas_call(
        _resid_norm_kernel, grid=(t // rows,),
        in_specs=[row_spec, row_spec, vec_spec, vec_spec],
        out_specs=[row_spec, row_spec],
        out_shape=[jax.ShapeDtypeStruct((t, d), F32), jax.ShapeDtypeStruct((t, d), next_dtype)],
        compiler_params=_params("parallel"), name="resid_norm",
    )(x, m, g_post.reshape(1, d), g_next.reshape(1, d))


class Weight(NamedTuple):
    array: jax.Array
    layer: int | None = None
    row_block: int = 0
    col0: int = 0
    n: int | None = None

    def cols(self):
        return self.array.shape[-1] if self.n is None else self.n

    def spec(self, k, tn):
        cb0 = self.col0 // tn
        if self.array.ndim == 3:
            return pl.BlockSpec((None, k, tn), lambda i, j: (self.layer, self.row_block, cb0 + j))
        return pl.BlockSpec((k, tn), lambda i, j: (self.row_block, cb0 + j))


def _mm_kernel(a_ref, w_ref, o_ref):
    o_ref[...] = _dot(a_ref[...], w_ref[...].astype(BF16)).astype(o_ref.dtype)


def _mm_scaled_kernel(a_ref, w_ref, o_ref, *, scale):
    o_ref[...] = (_dot(a_ref[...], w_ref[...].astype(BF16)) * scale).astype(o_ref.dtype)


def _mm_swiglu_kernel(a_ref, wg_ref, wu_ref, o_ref):
    a = a_ref[...]
    gate = _dot(a, wg_ref[...].astype(BF16))
    o_ref[...] = (_silu(gate) * _dot(a, wu_ref[...].astype(BF16))).astype(o_ref.dtype)


def _mm_glu_kernel(a_ref, wa_ref, wb_ref, o_ref):
    a = a_ref[...]
    lin = _dot(a, wa_ref[...].astype(BF16))
    o_ref[...] = (lin * jax.nn.sigmoid(_dot(a, wb_ref[...].astype(BF16)))).astype(o_ref.dtype)


def _matmul_call(body, a, weights, out_dtype, tm, tn, name):
    m, k = a.shape
    n = weights[0].cols()
    return pl.pallas_call(
        body,
        grid=(m // tm, n // tn),
        in_specs=[pl.BlockSpec((tm, k), lambda i, j: (i, 0))] + [w.spec(k, tn) for w in weights],
        out_specs=pl.BlockSpec((tm, tn), lambda i, j: (i, j)),
        out_shape=jax.ShapeDtypeStruct((m, n), out_dtype),
        compiler_params=_params("parallel", "arbitrary"),
        name=name,
    )(a, *[w.array for w in weights])


def matmul(a, w, out_dtype, tm, tn, name="matmul", scale=None):
    body = _mm_kernel if scale is None else functools.partial(_mm_scaled_kernel, scale=scale)
    return _matmul_call(body, a, (w,), out_dtype, tm, tn, name)


def _mm_nt_kernel(a_ref, wt_ref, o_ref):
    o_ref[...] = _dot_nt(wt_ref[...], a_ref[...]).astype(o_ref.dtype)


def matmul_nt(a, wt, out_dtype, tm, tn, name):
    m, k = a.shape
    n = wt.shape[0]
    return pl.pallas_call(
        _mm_nt_kernel,
        grid=(m // tm, n // tn),
        in_specs=[pl.BlockSpec((tm, k), lambda i, j: (i, 0)),
                  pl.BlockSpec((tn, k), lambda i, j: (j, 0))],
        out_specs=pl.BlockSpec((tn, tm), lambda i, j: (j, i)),
        out_shape=jax.ShapeDtypeStruct((n, m), out_dtype),
        compiler_params=_params("parallel", "arbitrary"),
        name=name,
    )(a, wt)


def _mm2_kernel(a1_ref, a2_ref, w1_ref, w2_ref, o_ref):
    o_ref[...] = (_dot(a1_ref[...], w1_ref[...].astype(BF16))
                  + _dot(a2_ref[...], w2_ref[...].astype(BF16))).astype(o_ref.dtype)


def matmul2(a1, a2, w1, w2, out_dtype, tm, tn, name):
    m, k = a1.shape
    assert a2.shape == a1.shape
    n = w1.cols()
    act = pl.BlockSpec((tm, k), lambda i, j: (i, 0))
    return pl.pallas_call(
        _mm2_kernel,
        grid=(m // tm, n // tn),
        in_specs=[act, act, w1.spec(k, tn), w2.spec(k, tn)],
        out_specs=pl.BlockSpec((tm, tn), lambda i, j: (i, j)),
        out_shape=jax.ShapeDtypeStruct((m, n), out_dtype),
        compiler_params=_params("parallel", "arbitrary"),
        name=name,
    )(a1, a2, w1.array, w2.array)


def matmul_swiglu(a, wg, wu, tm, tn):
    return _matmul_call(_mm_swiglu_kernel, a, (wg, wu), BF16, tm, tn, "ffn_gate_up")


def matmul_glu(a, wa, wb, tm, tn):
    return _matmul_call(_mm_glu_kernel, a, (wa, wb), BF16, tm, tn, "s5_glu")


def _gates_kernel(hn_ref, w_ref, prm_ref, o_ref, carry_ref, *, rows):
    @pl.when(pl.program_id(0) == 0)
    def _():
        carry_ref[...] = jnp.zeros_like(carry_ref)

    p = _dot(hn_ref[...], w_ref[...])
    lane = lax.broadcasted_iota(jnp.int32, (1, GATE_LANES), 1)
    is_decay = lane < GDN_HEADS
    is_beta = (lane >= GDN_HEADS) & (lane < 2 * GDN_HEADS)
    is_forget = (lane >= 2 * GDN_HEADS) & (lane < 2 * GDN_HEADS + FOX_HEADS)
    a_log, dt_bias, f_bias = prm_ref[0:1, :], prm_ref[1:2, :], prm_ref[2:3, :]
    decay = -jnp.exp(a_log) * _softplus(p + dt_bias)
    beta = jax.nn.sigmoid(p)
    log_f = -_softplus(-(p + f_bias))
    vals = jnp.where(is_decay, decay, jnp.where(is_forget, log_f, 0.0))

    r = lax.broadcasted_iota(jnp.int32, (GDN_CHUNK, GDN_CHUNK), 0)
    c = lax.broadcasted_iota(jnp.int32, (GDN_CHUNK, GDN_CHUNK), 1)
    tri = (r >= c).astype(BF16)
    carry = carry_ref[...]
    for ci in range(rows // GDN_CHUNK):
        sl = slice(ci * GDN_CHUNK, (ci + 1) * GDN_CHUNK)
        v = vals[sl]
        hi = v.astype(BF16)
        r1 = v - hi.astype(F32)
        mid = r1.astype(BF16)
        lo = (r1 - mid.astype(F32)).astype(BF16)
        cs = _dot(tri, hi) + _dot(tri, mid) + _dot(tri, lo)
        run = cs + carry
        o_ref[sl, :] = jnp.where(is_decay, cs, jnp.where(is_beta, beta[sl], jnp.where(is_forget, run, 0.0)))
        carry = jnp.where(is_forget, run[GDN_CHUNK - 1:GDN_CHUNK, :], 0.0)
    carry_ref[...] = carry


def gates(hn, w_gates, prm, rows=512):
    t, d = hn.shape
    return pl.pallas_call(
        functools.partial(_gates_kernel, rows=rows),
        grid=(t // rows,),
        in_specs=[pl.BlockSpec((rows, d), lambda i: (i, 0)),
                  pl.BlockSpec((d, GATE_LANES), lambda i: (0, 0)),
                  pl.BlockSpec((SUBLANES, GATE_LANES), lambda i: (0, 0))],
        out_specs=pl.BlockSpec((rows, GATE_LANES), lambda i: (i, 0)),
        out_shape=jax.ShapeDtypeStruct((t, GATE_LANES), F32),
        scratch_shapes=[pltpu.VMEM((1, GATE_LANES), F32)],
        compiler_params=_params("arbitrary"),
        name="gates",
    )(hn, w_gates, prm)


def _dot_split_parts(ah, al, bh, bl):
    return _dot(jnp.concatenate([ah, al], axis=1), jnp.concatenate([bh, bh], axis=0)) + _dot(ah, bl)


def _unit_lower_inverses(lmats):
    n = lmats[0].shape[0]
    r = lax.broadcasted_iota(jnp.int32, (n, n), 0)
    c = lax.broadcasted_iota(jnp.int32, (n, n), 1)
    zero = jnp.zeros((), BF16)
    lsplit = [_split2(l) for l in lmats]
    invs = [(r == c).astype(F32) for _ in lmats]
    shift = 0
    while (1 << shift) < n:
        rb = r >> shift
        joins = ((rb & 1) == 1) & ((c >> shift) == rb - 1)
        isplit = [_split2(inv) for inv in invs]
        prods = [_dot_split_parts(jnp.where(joins, lh, zero), jnp.where(joins, ll, zero), ih, il)
                 for (lh, ll), (ih, il) in zip(lsplit, isplit)]
        invs = [inv - _dot_split_parts(ih, il, *_split2(p))
                for inv, (ih, il), p in zip(invs, isplit, prods)]
        shift += 1
    return invs


def _gdn_kernel(q_ref, k_ref, v_ref, z_ref, qh_ref, kh_ref, vh_ref, cwq_ref, cwk_ref, cwv_ref,
                gcol_ref, grow_ref, onorm_ref, o_ref, s_ref, *, rows):
    h = pl.program_id(0)
    first = pl.program_id(1) == 0

    @pl.when(first)
    def _():
        s_ref[...] = jnp.zeros_like(s_ref)

    def conv_silu(x_ref, halo_ref, w_ref):
        x = x_ref[...]
        halo = jnp.where(first, 0.0, halo_ref[...])
        xe = jnp.concatenate([halo, x], axis=0)
        w = w_ref[...]
        y = w[CONV_WIDTH - 1:CONV_WIDTH, :] * x
        for i in range(CONV_WIDTH - 1):
            off = SUBLANES - (CONV_WIDTH - 1) + i
            y = y + w[i:i + 1, :] * xe[off:off + rows, :]
        return _silu(y)

    def l2norm(x):
        return x * lax.rsqrt(jnp.sum(x * x, axis=-1, keepdims=True) + L2_EPS)

    q = l2norm(conv_silu(q_ref, qh_ref, cwq_ref)) * (HEAD_DIM ** -0.5)
    k = l2norm(conv_silu(k_ref, kh_ref, cwk_ref))
    v = conv_silu(v_ref, vh_ref, cwv_ref)

    lane = lax.broadcasted_iota(jnp.int32, (1, GATE_LANES), 1)
    gates_col = gcol_ref[...]
    g_col = jnp.sum(jnp.where(lane == h, gates_col, 0.0), axis=1, keepdims=True)
    beta = jnp.sum(jnp.where(lane == h + GDN_HEADS, gates_col, 0.0), axis=1, keepdims=True)
    g_row = grow_ref[0]

    n = GDN_CHUNK
    r = lax.broadcasted_iota(jnp.int32, (n, n), 0)
    c = lax.broadcasted_iota(jnp.int32, (n, n), 1)
    chunks = [slice(ci * n, (ci + 1) * n) for ci in range(rows // n)]

    local = []
    for sl in chunks:
        qc, kc, vc, gc, bc = q[sl], k[sl], v[sl], g_col[sl], beta[sl]
        decay = jnp.exp(jnp.where(r >= c, gc - g_row[:, sl], NEG_INF))
        kb = kc * bc
        k16 = kc.astype(BF16)
        qk = _dot_nt(jnp.concatenate([kb, qc], axis=0).astype(BF16), k16) * jnp.tile(decay, (2, 1))
        eg = jnp.exp(gc)
        g_last = gc[n - 1:n, :]
        local.append(dict(
            lmat=jnp.where(r > c, qk[:n], 0.0), attn=qk[n:].astype(BF16),
            rhs=jnp.concatenate([vc * bc, kb * eg], axis=1), q_dec=(qc * eg).astype(BF16),
            k_dec=(kc * jnp.exp(g_last - gc)).astype(BF16), s_dec=jnp.exp(g_last)))
    invs = _unit_lower_inverses([lc["lmat"] for lc in local])
    for lc, inv in zip(local, invs):
        sol = _dot_split_parts(*_split2(inv), *_split2(lc["rhs"]))
        lc["u"], lc["w"] = sol[:, :HEAD_DIM], sol[:, HEAD_DIM:].astype(BF16)

    state = s_ref[...]
    for sl, lc in zip(chunks, local):
        s16 = state.astype(BF16)
        v_new = lc["u"] - _dot(lc["w"], s16)
        vn16 = v_new.astype(BF16)
        o = _dot(jnp.concatenate([lc["q_dec"], lc["attn"]], axis=1), jnp.concatenate([s16, vn16], axis=0))
        state = state * lc["s_dec"] + _dot_tn(lc["k_dec"], vn16)
        o_ref[sl, :] = (_rms(o, onorm_ref[...]) * _silu(z_ref[sl, :])).astype(o_ref.dtype)
    s_ref[...] = state


def gdn_mixer(proj_a, conv_w, gates_col, gates_row, o_norm, rows=512):
    t = proj_a.shape[0]
    nh = GDN_HEADS
    halo_blocks = rows // SUBLANES

    def col(section):
        return pl.BlockSpec((rows, HEAD_DIM), lambda h, i: (i, section * nh + h))

    def halo(section):
        return pl.BlockSpec((SUBLANES, HEAD_DIM),
                            lambda h, i: (jnp.maximum(i * halo_blocks - 1, 0), section * nh + h))

    def cw(section):
        return pl.BlockSpec((CONV_WIDTH, HEAD_DIM), lambda h, i: (0, section * nh + h))

    return pl.pallas_call(
        functools.partial(_gdn_kernel, rows=rows),
        grid=(nh, t // rows),
        in_specs=[col(0), col(1), col(2), col(3), halo(0), halo(1), halo(2), cw(0), cw(1), cw(2),
                  pl.BlockSpec((rows, GATE_LANES), lambda h, i: (i, 0)),
                  pl.BlockSpec((1, 1, rows), lambda h, i: (h, 0, i)),
                  pl.BlockSpec((1, HEAD_DIM), lambda h, i: (0, 0))],
        out_specs=pl.BlockSpec((rows, HEAD_DIM), lambda h, i: (i, h)),
        out_shape=jax.ShapeDtypeStruct((t, nh * HEAD_DIM), BF16),
        scratch_shapes=[pltpu.VMEM((HEAD_DIM, HEAD_DIM), F32)],
        compiler_params=_params("parallel", "arbitrary"),
        name="gdn",
    )(proj_a, proj_a, proj_a, proj_a, proj_a, proj_a, proj_a, conv_w, conv_w, conv_w,
      gates_col, gates_row, o_norm.reshape(1, HEAD_DIM))


FOX_Q_SCALE = LOG2E * HEAD_DIM ** -0.5


def _fox_kernel(q_ref, kt_ref, v_ref, grow_ref, o_ref, m_ref, l_ref, acc_ref, *, tq, wide):
    qi = pl.program_id(1)
    q0 = pl.multiple_of(qi * tq, tq)
    m_ref[...] = jnp.full_like(m_ref, NEG_INF)
    l_ref[...] = jnp.zeros_like(l_ref)
    acc_ref[...] = jnp.zeros_like(acc_ref)
    cum0 = grow_ref[0, :, pl.ds(q0, LANES)][:, 0:1]
    q = q_ref[...]

    def block(k0, width, diagonal):
        kt = kt_ref[:, pl.ds(k0, width)]
        vj = v_ref[pl.ds(k0, width), :]
        bias = (cum0 - grow_ref[0, :, pl.ds(k0, width)]) * LOG2E
        s = _dot(q, kt) + bias
        if diagonal:
            r = lax.broadcasted_iota(jnp.int32, (tq, width), 0)
            c = lax.broadcasted_iota(jnp.int32, (tq, width), 1)
            s = jnp.where(c <= r, s, NEG_INF)
        m_prev = m_ref[...]
        m_new = jnp.maximum(m_prev, jnp.max(s, axis=1, keepdims=True))
        alpha = jnp.exp2(m_prev - m_new)
        p = jnp.exp2(s - jnp.tile(m_new, (1, width // LANES)))
        p_sum = p[:, :LANES]
        for ct in range(1, width // LANES):
            p_sum = p_sum + p[:, ct * LANES:(ct + 1) * LANES]
        l_ref[...] = alpha * l_ref[...] + p_sum
        acc_ref[...] = alpha * acc_ref[...] + _dot(p.astype(BF16), vj)
        m_ref[...] = m_new

    per_wide = wide // tq
    n_wide = qi // per_wide

    def wide_block(j, carry):
        block(pl.multiple_of(j * wide, wide), wide, False)
        return carry

    lax.fori_loop(0, n_wide, wide_block, 0)
    for rem in range(per_wide - 1):
        @pl.when(qi - n_wide * per_wide > rem)
        def _():
            block(pl.multiple_of((n_wide * per_wide + rem) * tq, tq), tq, False)
    block(q0, tq, True)
    l = jnp.sum(l_ref[...], axis=1, keepdims=True)
    o_ref[...] = (acc_ref[...] / l).astype(o_ref.dtype)


def fox_attention(q, kt, v, gates_row, tq=512, wide=2048):
    t = q.shape[0]
    nh = FOX_HEADS
    wide = max(tq, min(wide, t))
    return pl.pallas_call(
        functools.partial(_fox_kernel, tq=tq, wide=wide),
        grid=(nh, t // tq),
        in_specs=[pl.BlockSpec((tq, HEAD_DIM), lambda h, qi: (qi, h)),
                  pl.BlockSpec((HEAD_DIM, t), lambda h, qi: (h, 0)),
                  pl.BlockSpec((t, HEAD_DIM), lambda h, qi: (0, h)),
                  pl.BlockSpec((1, 1, t), lambda h, qi: (2 * GDN_HEADS + h, 0, 0))],
        out_specs=pl.BlockSpec((tq, HEAD_DIM), lambda h, qi: (qi, h)),
        out_shape=jax.ShapeDtypeStruct((t, nh * HEAD_DIM), BF16),
        scratch_shapes=[pltpu.VMEM((tq, LANES), F32), pltpu.VMEM((tq, LANES), F32),
                        pltpu.VMEM((tq, HEAD_DIM), F32)],
        compiler_params=_params("parallel", "arbitrary"),
        name="fox",
    )(q, kt, v, gates_row)


S5_SCAN_LEVELS = 16


def _s5_setup_kernel(are_ref, aim_ref, ls_ref, bre_ref, bim_ref, cre_ref, cim_ref,
                     g_ref, ere_ref, eim_ref, pre_ref, pim_ref, sre_ref, sim_ref, *, gb):
    p, l = S5_GROUP, S5_CHUNK
    lam_re = jnp.minimum(are_ref[...], S5_MAX_RE)
    lam_im = aim_ref[...]
    dt = jnp.exp(ls_ref[...])
    mag = jnp.exp(lam_re * dt)
    a_re = mag * jnp.cos(lam_im * dt)
    a_im = mag * jnp.sin(lam_im * dt)
    nr, ni = a_re - 1.0, a_im
    den = lam_re * lam_re + lam_im * lam_im
    f_re = (nr * lam_re + ni * lam_im) / den
    f_im = (ni * lam_re - nr * lam_im) / den
    b_re, b_im = bre_ref[...], bim_ref[...]
    bb_re = f_re * b_re - f_im * b_im
    bb_im = f_re * b_im + f_im * b_re
    c_re, c_im = cre_ref[...], cim_ref[...]

    pows = [(jnp.ones_like(a_re), jnp.zeros_like(a_im))]
    for _ in range(l):
        pr, pi_ = pows[-1]
        pows.append((pr * a_re - pi_ * a_im, pr * a_im + pi_ * a_re))

    ca_re = [c_re * pr - c_im * pi_ for pr, pi_ in pows]
    ca_im = [c_re * pi_ + c_im * pr for pr, pi_ in pows]
    lo_re = jnp.concatenate(ca_re[:l], axis=1)
    lo_im = jnp.concatenate(ca_im[:l], axis=1)
    for gi in range(gb):
        g_ref[gi] = _dot_nt_f32(lo_re[gi], bb_re[gi]) - _dot_nt_f32(lo_im[gi], bb_im[gi])
    pre_ref[...] = jnp.concatenate(ca_re[1:], axis=1)
    pim_ref[...] = -jnp.concatenate(ca_im[1:], axis=1)
    ere_ref[...] = jnp.concatenate(
        [pows[l - 1 - j][0] * bb_re - pows[l - 1 - j][1] * bb_im for j in range(l)], axis=1)
    eim_ref[...] = jnp.concatenate(
        [pows[l - 1 - j][0] * bb_im + pows[l - 1 - j][1] * bb_re for j in range(l)], axis=1)
    sr, si = pows[l]
    for lv in range(S5_SCAN_LEVELS):
        sre_ref[:, lv:lv + 1, :] = sr
        sim_ref[:, lv:lv + 1, :] = si
        sr, si = sr * sr - si * si, 2.0 * sr * si


def s5_setup(a_re, a_im, log_step, b_re_t, b_im_t, c_re, c_im, gb=8):
    g, n = a_re.shape
    p, l = S5_GROUP, S5_CHUNK
    vec = pl.BlockSpec((gb, 1, n), lambda i: (i, 0, 0))
    mat = pl.BlockSpec((gb, p, n), lambda i: (i, 0, 0))
    big = pl.BlockSpec((gb, l * p, n), lambda i: (i, 0, 0))
    lvl = pl.BlockSpec((gb, S5_SCAN_LEVELS, n), lambda i: (i, 0, 0))
    big_shape = jax.ShapeDtypeStruct((g, l * p, n), F32)
    lvl_shape = jax.ShapeDtypeStruct((g, S5_SCAN_LEVELS, n), F32)
    return pl.pallas_call(
        functools.partial(_s5_setup_kernel, gb=gb),
        grid=(g // gb,),
        in_specs=[vec, vec, pl.BlockSpec((gb, 1, 1), lambda i: (i, 0, 0)), mat, mat, mat, mat],
        out_specs=[pl.BlockSpec((gb, l * p, p), lambda i: (i, 0, 0)), big, big, big, big, lvl, lvl],
        out_shape=[jax.ShapeDtypeStruct((g, l * p, p), F32), big_shape, big_shape, big_shape, big_shape,
                   lvl_shape, lvl_shape],
        compiler_params=_params("parallel"),
        name="s5_setup",
    )(a_re.reshape(g, 1, n), a_im.reshape(g, 1, n), log_step.reshape(g, 1, 1), b_re_t, b_im_t, c_re, c_im)


S5_LANE_GROUPS = LANES // S5_GROUP


def _s5_kernel(u_ref, rk_ref, re_ref, rp_ref, s1_ref, s2_ref, d_ref, o_ref, y_ref, *, levels):
    l, p, gl = S5_CHUNK, S5_GROUP, S5_LANE_GROUPS
    nst = 2 * S5_STATE
    nc = u_ref.shape[0] // l
    u = [u_ref[pl.ds(j, nc, stride=l), :] for j in range(l)]
    u16 = [x.astype(BF16) for x in u]

    def block_diag(rows_per_group, cols_per_group, compressed):
        shape = (gl * rows_per_group, gl * cols_per_group)
        r = lax.broadcasted_iota(jnp.int32, shape, 0) >> int(math.log2(rows_per_group))
        c = lax.broadcasted_iota(jnp.int32, shape, 1) >> int(math.log2(cols_per_group))
        return jnp.where(r == c, jnp.tile(compressed, (gl, 1)), 0.0).astype(BF16)

    e_bd = jnp.concatenate([block_diag(p, nst, re_ref[j]) for j in range(l)], axis=0)
    x = _dot(jnp.concatenate(u16, axis=1), e_bd)
    row = lax.broadcasted_iota(jnp.int32, x.shape, 0)

    def swap_halves(v):
        return jnp.concatenate([pltpu.roll(v[:, g * nst:(g + 1) * nst], S5_STATE, axis=1) for g in range(gl)], axis=1)

    for lv in range(levels):
        sh = 1 << lv
        xs = jnp.where(row >= sh, pltpu.roll(x, sh, axis=0), 0.0)
        x = x + s1_ref[lv:lv + 1, :] * xs + s2_ref[lv:lv + 1, :] * swap_halves(xs)
    s_in = jnp.where(row >= 1, pltpu.roll(x, 1, axis=0), 0.0).astype(BF16)

    k_bd = [block_diag(p, p, rk_ref[k]) for k in range(l)]
    zero_bd = jnp.zeros((LANES, LANES), BF16)
    for j in range(0, l, 2):
        lhs = jnp.concatenate(u16[:j + 2] + [s_in], axis=1)
        rhs_rows = [jnp.concatenate([k_bd[j - i] if i <= j else zero_bd, k_bd[j + 1 - i]], axis=1)
                    for i in range(j + 2)]
        rhs_rows.append(jnp.concatenate([block_diag(nst, p, rp_ref[j]), block_diag(nst, p, rp_ref[j + 1])], axis=1))
        y = _dot(lhs, jnp.concatenate(rhs_rows, axis=0))
        for jj in range(2):
            yj = y[:, jj * LANES:(jj + 1) * LANES] + d_ref[...] * u[j + jj]
            y_ref[pl.ds(j + jj, nc, stride=l), :] = jax.nn.gelu(yj)
    o_ref[...] = y_ref[...].astype(o_ref.dtype)


def s5_scan(hn, rk, re, rp, s1, s2, d_skip):
    t, d = hn.shape
    l, p, gl = S5_CHUNK, S5_GROUP, S5_LANE_GROUPS
    nst = 2 * S5_STATE
    levels = max(1, math.ceil(math.log2(t // l)))
    assert levels <= S5_SCAN_LEVELS

    def per_tile(*shape):
        return pl.BlockSpec((None,) + shape, lambda i: (i,) + (0,) * len(shape))

    col = pl.BlockSpec((t, LANES), lambda i: (0, i))
    return pl.pallas_call(
        functools.partial(_s5_kernel, levels=levels),
        grid=(d // LANES,),
        in_specs=[col, per_tile(l, p, LANES), per_tile(l, p, gl * nst), per_tile(l, nst, LANES),
                  per_tile(S5_SCAN_LEVELS, gl * nst), per_tile(S5_SCAN_LEVELS, gl * nst),
                  pl.BlockSpec((1, LANES), lambda i: (0, i))],
        out_specs=col,
        out_shape=jax.ShapeDtypeStruct((t, d), BF16),
        scratch_shapes=[pltpu.VMEM((t, LANES), F32)],
        compiler_params=_params("parallel"),
        name="s5_scan",
    )(hn, rk, re, rp, s1, s2, d_skip.reshape(1, d))


def s5_mixer(hn, a_re, a_im, log_step, b_re, b_im, c_re, c_im, d_skip):
    t, d = hn.shape
    g, n = a_re.shape
    p, l = S5_GROUP, S5_CHUNK
    nc = t // l
    gm, e_re, e_im, p_re, p_im, s_re, s_im = s5_setup(
        a_re, a_im, log_step, b_re.transpose(0, 2, 1), b_im.transpose(0, 2, 1), c_re, c_im)
    gl, nt, nst = S5_LANE_GROUPS, g // S5_LANE_GROUPS, 2 * n
    rk = gm.reshape(nt, gl, l, p, p).transpose(0, 2, 4, 1, 3).reshape(nt, l, p, gl * p)
    emat = jnp.concatenate([e_re, e_im], axis=-1)
    re = emat.reshape(nt, gl, l, p, nst).transpose(0, 2, 3, 1, 4).reshape(nt, l, p, gl * nst)
    pmat = jnp.concatenate([p_re, p_im], axis=-1)
    rp = pmat.reshape(nt, gl, l, p, nst).transpose(0, 2, 4, 1, 3).reshape(nt, l, nst, gl * p)

    def per_tile(v):
        return v.reshape(nt, gl, S5_SCAN_LEVELS, nst).transpose(0, 2, 1, 3).reshape(nt, S5_SCAN_LEVELS, gl * nst)

    s1 = per_tile(jnp.concatenate([s_re, s_re], axis=-1))
    s2 = per_tile(jnp.concatenate([-s_im, s_im], axis=-1))
    return s5_scan(hn, rk, re, rp, s1, s2, d_skip)


TILES = {"proj": (1024, 512), "gate_up": (1024, 256), "glu": (1024, 256), "down": (512, 256), "out": (1024, 512)}


def _tile(kind, m):
    tm, tn = TILES[kind]
    return dict(tm=min(tm, m), tn=tn)


def _ffn(hn, w_gate, w_up, w_down, layer):
    m = hn.shape[0]
    hidden = matmul_swiglu(hn, Weight(w_gate, layer), Weight(w_up, layer), **_tile("gate_up", m))
    return matmul(hidden, Weight(w_down[layer].astype(BF16)), BF16, name="ffn_down", **_tile("down", m))


def kernel(x, norm_g, w_in, gdn_conv_w, gdn_A_log, gdn_dt_bias, gdn_o_norm, fox_f_bias, w_out, s5_A_re, s5_A_im, s5_log_step, s5_B_re, s5_B_im, s5_C_re, s5_C_im, s5_D, s5_w_glu_a, s5_w_glu_b, ffn_w_gate, ffn_w_up, ffn_w_down):
    bsz, t, d = x.shape
    assert bsz == 1
    x = x.reshape(t, d)
    gw = GDN_HEADS * HEAD_DIM
    fw = FOX_HEADS * HEAD_DIM
    proj = _tile("proj", t)

    w0 = w_in[0]
    o_gate = 4 * gw
    o_fox = o_gate + 2 * GDN_HEADS
    o_forget = o_fox + 3 * fw
    w_fq = w0[:, o_fox:o_fox + fw].astype(BF16)
    w_fkt = w0[:, o_fox + fw:o_fox + 2 * fw].T.astype(BF16)
    w_fv = w0[:, o_fox + 2 * fw:o_forget].astype(BF16)
    pad = GATE_LANES - 2 * GDN_HEADS - FOX_HEADS
    w_g = jnp.concatenate([w0[:, o_gate:o_fox], w0[:, o_forget:], jnp.zeros((d, pad), F32)], axis=1).astype(BF16)
    prm = jnp.zeros((SUBLANES, GATE_LANES), F32)
    prm = prm.at[0, :GDN_HEADS].set(gdn_A_log[0]).at[1, :GDN_HEADS].set(gdn_dt_bias[0])
    prm = prm.at[2, 2 * GDN_HEADS:2 * GDN_HEADS + FOX_HEADS].set(fox_f_bias[0])

    hn = rmsnorm(x, norm_g[0, 0], BF16)
    proj_a = matmul(hn, Weight(w_in, 0, n=o_gate), F32, name="proj_gdn", **proj)
    fox_q = matmul(hn, Weight(w_fq), BF16, name="proj_fox_q", scale=FOX_Q_SCALE, **proj)
    fox_kt = matmul_nt(hn, w_fkt, BF16, name="proj_fox_kt", **proj)
    fox_v = matmul(hn, Weight(w_fv), BF16, name="proj_fox_v", **proj)
    gates_col = gates(hn, w_g, prm, rows=min(512, t))
    gates_row = gates_col.T.reshape(GATE_LANES, 1, t)
    o_a = gdn_mixer(proj_a, gdn_conv_w[0], gates_col, gates_row, gdn_o_norm[0])
    o_b = fox_attention(fox_q, fox_kt, fox_v, gates_row, tq=min(512, t))
    m = matmul2(o_a, o_b, Weight(w_out, 0, row_block=0), Weight(w_out, 0, row_block=1), BF16,
                name="out_proj", **_tile("out", t))
    x, hn = resid_norm(x, m, norm_g[0, 1], norm_g[0, 2], BF16)
    f = _ffn(hn, ffn_w_gate, ffn_w_up, ffn_w_down, 0)
    x, hn = resid_norm(x, f, norm_g[0, 3], norm_g[1, 0], F32)

    y = s5_mixer(hn, s5_A_re[0], s5_A_im[0], s5_log_step[0], s5_B_re[0], s5_B_im[0],
                 s5_C_re[0], s5_C_im[0], s5_D[0])
    m = matmul_glu(y, Weight(s5_w_glu_a, 0), Weight(s5_w_glu_b, 0), **_tile("glu", t))
    x, hn = resid_norm(x, m, norm_g[1, 1], norm_g[1, 2], BF16)
    f = _ffn(hn, ffn_w_gate, ffn_w_up, ffn_w_down, 1)
    x = resid_norm(x, f, norm_g[1, 3], None, None)
    return x.reshape(bsz, t, d)
```

```python
import functools
import math
from typing import NamedTuple

import jax
import jax.numpy as jnp
from jax import lax
from jax.experimental import pallas as pl
from jax.experimental.pallas import tpu as pltpu

F32 = jnp.float32
BF16 = jnp.bfloat16

LANES = 128
SUBLANES = 8
VMEM_LIMIT_BYTES = 56 * 1024 * 1024

HEAD_DIM = 128
GDN_HEADS = 16
FOX_HEADS = 16
CONV_WIDTH = 4
GDN_CHUNK = 128
S5_GROUP = 16
S5_STATE = 64
S5_CHUNK = 16
S5_MAX_RE = -1e-4
RMS_EPS = 1e-6
L2_EPS = 1e-6
GATE_LANES = LANES
NEG_INF = float("-inf")
LOG2E = math.log2(math.e)


def _params(*semantics):
    return pltpu.CompilerParams(dimension_semantics=semantics,
                                vmem_limit_bytes=VMEM_LIMIT_BYTES)


def _silu(x):
    return x * jax.nn.sigmoid(x)


def _softplus(x):
    return jnp.maximum(x, 0.0) + jnp.log1p(jnp.exp(-jnp.abs(x)))


def _dot(a, b):
    return jnp.dot(a, b, preferred_element_type=F32)


def _dot_nt(a, b):
    return lax.dot_general(a, b, (((1,), (1,)), ((), ())), preferred_element_type=F32)


def _dot_nt_f32(a, b):
    return lax.dot_general(a, b, (((1,), (1,)), ((), ())), precision=lax.Precision.HIGHEST,
                           preferred_element_type=F32)


def _dot_tn(a, b):
    return lax.dot_general(a, b, (((0,), (0,)), ((), ())), preferred_element_type=F32)


def _split2(x):
    hi = x.astype(BF16)
    lo = (x - hi.astype(F32)).astype(BF16)
    return hi, lo


def _rms(x, g):
    y = x * lax.rsqrt(jnp.mean(x * x, axis=-1, keepdims=True) + RMS_EPS)
    return y * g


def _rmsnorm_kernel(x_ref, g_ref, o_ref):
    o_ref[...] = _rms(x_ref[...], g_ref[...]).astype(o_ref.dtype)


def rmsnorm(x, g, out_dtype, rows=256):
    t, d = x.shape
    return pl.pallas_call(
        _rmsnorm_kernel,
        grid=(t // rows,),
        in_specs=[pl.BlockSpec((rows, d), lambda i: (i, 0)),
                  pl.BlockSpec((1, d), lambda i: (0, 0))],
        out_specs=pl.BlockSpec((rows, d), lambda i: (i, 0)),
        out_shape=jax.ShapeDtypeStruct((t, d), out_dtype),
        compiler_params=_params("parallel"),
        name="rmsnorm",
    )(x, g.reshape(1, d))


def _resid_norm_kernel(x_ref, m_ref, gp_ref, gn_ref, xo_ref, ho_ref):
    xn = x_ref[...] + _rms(m_ref[...].astype(F32), gp_ref[...])
    xo_ref[...] = xn
    ho_ref[...] = _rms(xn, gn_ref[...]).astype(ho_ref.dtype)


def _resid_kernel(x_ref, m_ref, gp_ref, xo_ref):
    xo_ref[...] = x_ref[...] + _rms(m_ref[...].astype(F32), gp_ref[...])


def resid_norm(x, m, g_post, g_next, next_dtype, rows=256):
    t, d = x.shape
    row_spec = pl.BlockSpec((rows, d), lambda i: (i, 0))
    vec_spec = pl.BlockSpec((1, d), lambda i: (0, 0))
    if g_next is None:
        return pl.pallas_call(
            _resid_kernel, grid=(t // rows,),
            in_specs=[row_spec, row_spec, vec_spec], out_specs=row_spec,
            out_shape=jax.ShapeDtypeStruct((t, d), F32),
            compiler_params=_params("parallel"), name="resid",
        )(x, m, g_post.reshape(1, d))
    return pl.pallas_call(
        _resid_norm_kernel, grid=(t // rows,),
        in_specs=[row_spec, row_spec, vec_spec, vec_spec],
        out_specs=[row_spec, row_spec],
        out_shape=[jax.ShapeDtypeStruct((t, d), F32), jax.ShapeDtypeStruct((t, d), next_dtype)],
        compiler_params=_params("parallel"), name="resid_norm",
    )(x, m, g_post.reshape(1, d), g_next.reshape(1, d))


class Weight(NamedTuple):
    array: jax.Array
    layer: int | None = None
    row_block: int = 0
    col0: int = 0
    n: int | None = None

    def cols(self):
        return self.array.shape[-1] if self.n is None else self.n

    def spec(self, k, tn):
        cb0 = self.col0 // tn
        if self.array.ndim == 3:
            return pl.BlockSpec((None, k, tn), lambda i, j: (self.layer, self.row_block, cb0 + j))
        return pl.BlockSpec((k, tn), lambda i, j: (self.row_block, cb0 + j))


def _mm_kernel(a_ref, w_ref, o_ref):
    o_ref[...] = _dot(a_ref[...], w_ref[...].astype(BF16)).astype(o_ref.dtype)


def _mm_scaled_kernel(a_ref, w_ref, o_ref, *, scale):
    o_ref[...] = (_dot(a_ref[...], w_ref[...].astype(BF16)) * scale).astype(o_ref.dtype)


def _mm_swiglu_kernel(a_ref, wg_ref, wu_ref, o_ref):
    a = a_ref[...]
    gate = _dot(a, wg_ref[...].astype(BF16))
    o_ref[...] = (_silu(gate) * _dot(a, wu_ref[...].astype(BF16))).astype(o_ref.dtype)


def _mm_glu_kernel(a_ref, wa_ref, wb_ref, o_ref):
    a = a_ref[...]
    lin = _dot(a, wa_ref[...].astype(BF16))
    o_ref[...] = (lin * jax.nn.sigmoid(_dot(a, wb_ref[...].astype(BF16)))).astype(o_ref.dtype)


def _matmul_call(body, a, weights, out_dtype, tm, tn, name):
    m, k = a.shape
    n = weights[0].cols()
    return pl.pallas_call(
        body,
        grid=(m // tm, n // tn),
        in_specs=[pl.BlockSpec((tm, k), lambda i, j: (i, 0))] + [w.spec(k, tn) for w in weights],
        out_specs=pl.BlockSpec((tm, tn), lambda i, j: (i, j)),
        out_shape=jax.ShapeDtypeStruct((m, n), out_dtype),
        compiler_params=_params("parallel", "arbitrary"),
        name=name,
    )(a, *[w.array for w in weights])


def matmul(a, w, out_dtype, tm, tn, name="matmul", scale=None):
    body = _mm_kernel if scale is None else functools.partial(_mm_scaled_kernel, scale=scale)
    return _matmul_call(body, a, (w,), out_dtype, tm, tn, name)


def _mm_nt_kernel(a_ref, wt_ref, o_ref):
    o_ref[...] = _dot_nt(wt_ref[...], a_ref[...]).astype(o_ref.dtype)


def matmul_nt(a, wt, out_dtype, tm, tn, name):
    m, k = a.shape
    n = wt.shape[0]
    return pl.pallas_call(
        _mm_nt_kernel,
        grid=(m // tm, n // tn),
        in_specs=[pl.BlockSpec((tm, k), lambda i, j: (i, 0)),
                  pl.BlockSpec((tn, k), lambda i, j: (j, 0))],
        out_specs=pl.BlockSpec((tn, tm), lambda i, j: (j, i)),
        out_shape=jax.ShapeDtypeStruct((n, m), out_dtype),
        compiler_params=_params("parallel", "arbitrary"),
        name=name,
    )(a, wt)


def _mm2_kernel(a1_ref, a2_ref, w1_ref, w2_ref, o_ref):
    o_ref[...] = (_dot(a1_ref[...], w1_ref[...].astype(BF16))
                  + _dot(a2_ref[...], w2_ref[...].astype(BF16))).astype(o_ref.dtype)


def matmul2(a1, a2, w1, w2, out_dtype, tm, tn, name):
    m, k = a1.shape
    assert a2.shape == a1.shape
    n = w1.cols()
    act = pl.BlockSpec((tm, k), lambda i, j: (i, 0))
    return pl.pallas_call(
        _mm2_kernel,
        grid=(m // tm, n // tn),
        in_specs=[act, act, w1.spec(k, tn), w2.spec(k, tn)],
        out_specs=pl.BlockSpec((tm, tn), lambda i, j: (i, j)),
        out_shape=jax.ShapeDtypeStruct((m, n), out_dtype),
        compiler_params=_params("parallel", "arbitrary"),
        name=name,
    )(a1, a2, w1.array, w2.array)


def _mm_swiglu_cast_kernel(a_ref, wg_ref, wu_ref, wd_ref, o_ref, wd16_ref):
    _mm_swiglu_kernel(a_ref, wg_ref, wu_ref, o_ref)
    wd16_ref[...] = wd_ref[...].astype(BF16)


def matmul_swiglu(a, wg, wu, w_down, tm, tn):
    m, k = a.shape
    n = wg.cols()
    steps_m, steps_n = m // tm, n // tn
    f, d = w_down.array.shape[-2:]
    slab = f // (steps_m * steps_n)
    assert slab * steps_m * steps_n == f and slab % (2 * SUBLANES) == 0
    return pl.pallas_call(
        _mm_swiglu_cast_kernel,
        grid=(steps_m, steps_n),
        in_specs=[pl.BlockSpec((tm, k), lambda i, j: (i, 0)), wg.spec(k, tn), wu.spec(k, tn),
                  pl.BlockSpec((None, slab, d), lambda i, j: (w_down.layer, i * steps_n + j, 0))],
        out_specs=[pl.BlockSpec((tm, tn), lambda i, j: (i, j)),
                   pl.BlockSpec((slab, d), lambda i, j: (i * steps_n + j, 0))],
        out_shape=[jax.ShapeDtypeStruct((m, n), BF16), jax.ShapeDtypeStruct((f, d), BF16)],
        compiler_params=_params("parallel", "arbitrary"),
        name="ffn_gate_up",
    )(a, wg.array, wu.array, w_down.array)


def matmul_glu(a, wa, wb, tm, tn):
    return _matmul_call(_mm_glu_kernel, a, (wa, wb), BF16, tm, tn, "s5_glu")


def _gates_kernel(hn_ref, w_ref, prm_ref, o_ref, carry_ref, *, rows):
    @pl.when(pl.program_id(0) == 0)
    def _():
        carry_ref[...] = jnp.zeros_like(carry_ref)

    p = _dot(hn_ref[...], w_ref[...])
    lane = lax.broadcasted_iota(jnp.int32, (1, GATE_LANES), 1)
    is_decay = lane < GDN_HEADS
    is_beta = (lane >= GDN_HEADS) & (lane < 2 * GDN_HEADS)
    is_forget = (lane >= 2 * GDN_HEADS) & (lane < 2 * GDN_HEADS + FOX_HEADS)
    a_log, dt_bias, f_bias = prm_ref[0:1, :], prm_ref[1:2, :], prm_ref[2:3, :]
    decay = -jnp.exp(a_log) * _softplus(p + dt_bias)
    beta = jax.nn.sigmoid(p)
    log_f = -_softplus(-(p + f_bias))
    vals = jnp.where(is_decay, decay, jnp.where(is_forget, log_f, 0.0))

    r = lax.broadcasted_iota(jnp.int32, (GDN_CHUNK, GDN_CHUNK), 0)
    c = lax.broadcasted_iota(jnp.int32, (GDN_CHUNK, GDN_CHUNK), 1)
    tri = (r >= c).astype(BF16)
    carry = carry_ref[...]
    for ci in range(rows // GDN_CHUNK):
        sl = slice(ci * GDN_CHUNK, (ci + 1) * GDN_CHUNK)
        v = vals[sl]
        hi = v.astype(BF16)
        r1 = v - hi.astype(F32)
        mid = r1.astype(BF16)
        lo = (r1 - mid.astype(F32)).astype(BF16)
        cs = _dot(tri, hi) + _dot(tri, mid) + _dot(tri, lo)
        run = cs + carry
        o_ref[sl, :] = jnp.where(is_decay, cs, jnp.where(is_beta, beta[sl], jnp.where(is_forget, run, 0.0)))
        carry = jnp.where(is_forget, run[GDN_CHUNK - 1:GDN_CHUNK, :], 0.0)
    carry_ref[...] = carry


def gates(hn, w_gates, prm, rows=512):
    t, d = hn.shape
    return pl.pallas_call(
        functools.partial(_gates_kernel, rows=rows),
        grid=(t // rows,),
        in_specs=[pl.BlockSpec((rows, d), lambda i: (i, 0)),
                  pl.BlockSpec((d, GATE_LANES), lambda i: (0, 0)),
                  pl.BlockSpec((SUBLANES, GATE_LANES), lambda i: (0, 0))],
        out_specs=pl.BlockSpec((rows, GATE_LANES), lambda i: (i, 0)),
        out_shape=jax.ShapeDtypeStruct((t, GATE_LANES), F32),
        scratch_shapes=[pltpu.VMEM((1, GATE_LANES), F32)],
        compiler_params=_params("arbitrary"),
        name="gates",
    )(hn, w_gates, prm)


def _dot_split_parts(ah, al, bh, bl):
    return _dot(jnp.concatenate([ah, al], axis=1), jnp.concatenate([bh, bh], axis=0)) + _dot(ah, bl)


def _unit_lower_inverses(lmats):
    n = lmats[0].shape[0]
    r = lax.broadcasted_iota(jnp.int32, (n, n), 0)
    c = lax.broadcasted_iota(jnp.int32, (n, n), 1)
    zero = jnp.zeros((), BF16)
    lsplit = [_split2(l) for l in lmats]
    invs = [(r == c).astype(F32) for _ in lmats]
    shift = 0
    while (1 << shift) < n:
        rb = r >> shift
        joins = ((rb & 1) == 1) & ((c >> shift) == rb - 1)
        isplit = [_split2(inv) for inv in invs]
        prods = [_dot_split_parts(jnp.where(joins, lh, zero), jnp.where(joins, ll, zero), ih, il)
                 for (lh, ll), (ih, il) in zip(lsplit, isplit)]
        invs = [inv - _dot_split_parts(ih, il, *_split2(p))
                for inv, (ih, il), p in zip(invs, isplit, prods)]
        shift += 1
    return invs


def _gdn_kernel(q_ref, k_ref, v_ref, z_ref, qh_ref, kh_ref, vh_ref, cwq_ref, cwk_ref, cwv_ref,
                gcol_ref, grow_ref, onorm_ref, o_ref, s_ref, *, rows, hb):
    head0 = pl.program_id(0) * hb
    first = pl.program_id(1) == 0

    @pl.when(first)
    def _():
        s_ref[...] = jnp.zeros_like(s_ref)

    def conv_silu(x_ref, halo_ref, w_ref, cols):
        x = x_ref[:, cols]
        halo = jnp.where(first, 0.0, halo_ref[:, cols])
        xe = jnp.concatenate([halo, x], axis=0)
        w = w_ref[:, cols]
        y = w[CONV_WIDTH - 1:CONV_WIDTH, :] * x
        for i in range(CONV_WIDTH - 1):
            off = SUBLANES - (CONV_WIDTH - 1) + i
            y = y + w[i:i + 1, :] * xe[off:off + rows, :]
        return _silu(y)

    def l2norm(x):
        return x * lax.rsqrt(jnp.sum(x * x, axis=-1, keepdims=True) + L2_EPS)

    n = GDN_CHUNK
    r = lax.broadcasted_iota(jnp.int32, (n, n), 0)
    c = lax.broadcasted_iota(jnp.int32, (n, n), 1)
    lane = lax.broadcasted_iota(jnp.int32, (1, GATE_LANES), 1)
    gates_col = gcol_ref[...]
    chunks = [slice(ci * n, (ci + 1) * n) for ci in range(rows // n)]

    local = []
    for e in range(hb):
        cols = slice(e * HEAD_DIM, (e + 1) * HEAD_DIM)
        q = l2norm(conv_silu(q_ref, qh_ref, cwq_ref, cols)) * (HEAD_DIM ** -0.5)
        k = l2norm(conv_silu(k_ref, kh_ref, cwk_ref, cols))
        v = conv_silu(v_ref, vh_ref, cwv_ref, cols)
        g_col = jnp.sum(jnp.where(lane == head0 + e, gates_col, 0.0), axis=1, keepdims=True)
        beta = jnp.sum(jnp.where(lane == head0 + e + GDN_HEADS, gates_col, 0.0), axis=1, keepdims=True)
        g_row = grow_ref[e]
        for sl in chunks:
            qc, kc, vc, gc, bc = q[sl], k[sl], v[sl], g_col[sl], beta[sl]
            decay = jnp.exp(jnp.where(r >= c, gc - g_row[:, sl], NEG_INF))
            kb = kc * bc
            k16 = kc.astype(BF16)
            qk = _dot_nt(jnp.concatenate([kb, qc], axis=0).astype(BF16), k16) * jnp.tile(decay, (2, 1))
            eg = jnp.exp(gc)
            g_last = gc[n - 1:n, :]
            local.append(dict(
                lmat=jnp.where(r > c, qk[:n], 0.0), attn=qk[n:].astype(BF16),
                rhs=jnp.concatenate([vc * bc, kb * eg], axis=1), q_dec=(qc * eg).astype(BF16),
                k_dec=(kc * jnp.exp(g_last - gc)).astype(BF16), s_dec=jnp.exp(g_last)))
    invs = _unit_lower_inverses([lc["lmat"] for lc in local])
    for lc, inv in zip(local, invs):
        sol = _dot_split_parts(*_split2(inv), *_split2(lc["rhs"]))
        lc["u"], lc["w"] = sol[:, :HEAD_DIM], sol[:, HEAD_DIM:].astype(BF16)

    states = [s_ref[e] for e in range(hb)]
    for ci, sl in enumerate(chunks):
        for e in range(hb):
            lc = local[e * len(chunks) + ci]
            cols = slice(e * HEAD_DIM, (e + 1) * HEAD_DIM)
            s16 = states[e].astype(BF16)
            v_new = lc["u"] - _dot(lc["w"], s16)
            vn16 = v_new.astype(BF16)
            o = _dot(jnp.concatenate([lc["q_dec"], lc["attn"]], axis=1), jnp.concatenate([s16, vn16], axis=0))
            states[e] = states[e] * lc["s_dec"] + _dot_tn(lc["k_dec"], vn16)
            o_ref[sl, cols] = (_rms(o, onorm_ref[...]) * _silu(z_ref[sl, cols])).astype(o_ref.dtype)
    for e in range(hb):
        s_ref[e] = states[e]


def gdn_mixer(proj_a, conv_w, gates_col, gates_row, o_norm, rows=512, hb=2):
    t = proj_a.shape[0]
    nhb = GDN_HEADS // hb
    width = hb * HEAD_DIM
    halo_blocks = rows // SUBLANES

    def col(section):
        return pl.BlockSpec((rows, width), lambda h, i: (i, section * nhb + h))

    def halo(section):
        return pl.BlockSpec((SUBLANES, width),
                            lambda h, i: (jnp.maximum(i * halo_blocks - 1, 0), section * nhb + h))

    def cw(section):
        return pl.BlockSpec((CONV_WIDTH, width), lambda h, i: (0, section * nhb + h))

    return pl.pallas_call(
        functools.partial(_gdn_kernel, rows=rows, hb=hb),
        grid=(nhb, t // rows),
        in_specs=[col(0), col(1), col(2), col(3), halo(0), halo(1), halo(2), cw(0), cw(1), cw(2),
                  pl.BlockSpec((rows, GATE_LANES), lambda h, i: (i, 0)),
                  pl.BlockSpec((hb, 1, rows), lambda h, i: (h, 0, i)),
                  pl.BlockSpec((1, HEAD_DIM), lambda h, i: (0, 0))],
        out_specs=pl.BlockSpec((rows, width), lambda h, i: (i, h)),
        out_shape=jax.ShapeDtypeStruct((t, GDN_HEADS * HEAD_DIM), BF16),
        scratch_shapes=[pltpu.VMEM((hb, HEAD_DIM, HEAD_DIM), F32)],
        compiler_params=_params("parallel", "arbitrary"),
        name="gdn",
    )(proj_a, proj_a, proj_a, proj_a, proj_a, proj_a, proj_a, conv_w, conv_w, conv_w,
      gates_col, gates_row, o_norm.reshape(1, HEAD_DIM))


FOX_Q_SCALE = LOG2E * HEAD_DIM ** -0.5


def _fox_kernel(q_ref, kt_ref, v_ref, grow_ref, o_ref, m_ref, l_ref, acc_ref, *, tq, wide, hb):
    qi = pl.program_id(1)
    q0 = pl.multiple_of(qi * tq, tq)
    m_ref[...] = jnp.full_like(m_ref, NEG_INF)
    l_ref[...] = jnp.zeros_like(l_ref)
    acc_ref[...] = jnp.zeros_like(acc_ref)
    heads = [slice(e * HEAD_DIM, (e + 1) * HEAD_DIM) for e in range(hb)]
    cum0 = [grow_ref[e, :, pl.ds(q0, LANES)][:, 0:1] for e in range(hb)]
    q = [q_ref[:, hd] for hd in heads]

    def block(k0, width, diagonal):
        for e, hd in enumerate(heads):
            kt = kt_ref[hd, pl.ds(k0, width)]
            vj = v_ref[pl.ds(k0, width), hd]
            bias = (cum0[e] - grow_ref[e, :, pl.ds(k0, width)]) * LOG2E
            s = _dot(q[e], kt) + bias
            if diagonal:
                r = lax.broadcasted_iota(jnp.int32, (tq, width), 0)
                c = lax.broadcasted_iota(jnp.int32, (tq, width), 1)
                s = jnp.where(c <= r, s, NEG_INF)
            m_prev = m_ref[e]
            m_new = jnp.maximum(m_prev, jnp.max(s, axis=1, keepdims=True))
            alpha = jnp.exp2(m_prev - m_new)
            p = jnp.exp2(s - jnp.tile(m_new, (1, width // LANES)))
            p_sum = p[:, :LANES]
            for ct in range(1, width // LANES):
                p_sum = p_sum + p[:, ct * LANES:(ct + 1) * LANES]
            l_ref[e] = alpha * l_ref[e] + p_sum
            acc_ref[e] = alpha * acc_ref[e] + _dot(p.astype(BF16), vj)
            m_ref[e] = m_new

    per_wide = wide // tq
    n_wide = qi // per_wide

    def wide_block(j, carry):
        block(pl.multiple_of(j * wide, wide), wide, False)
        return carry

    lax.fori_loop(0, n_wide, wide_block, 0)
    for rem in range(per_wide - 1):
        @pl.when(qi - n_wide * per_wide > rem)
        def _():
            block(pl.multiple_of((n_wide * per_wide + rem) * tq, tq), tq, False)
    block(q0, tq, True)
    for e, hd in enumerate(heads):
        l = jnp.sum(l_ref[e], axis=1, keepdims=True)
        o_ref[:, hd] = (acc_ref[e] / l).astype(o_ref.dtype)


def fox_attention(q, kt, v, gates_row, tq=512, wide=2048, hb=2):
    t = q.shape[0]
    nhb = FOX_HEADS // hb
    width = hb * HEAD_DIM
    gate_block0 = 2 * GDN_HEADS // hb
    wide = max(tq, min(wide, t))
    return pl.pallas_call(
        functools.partial(_fox_kernel, tq=tq, wide=wide, hb=hb),
        grid=(nhb, t // tq),
        in_specs=[pl.BlockSpec((tq, width), lambda h, qi: (qi, h)),
                  pl.BlockSpec((width, t), lambda h, qi: (h, 0)),
                  pl.BlockSpec((t, width), lambda h, qi: (0, h)),
                  pl.BlockSpec((hb, 1, t), lambda h, qi: (gate_block0 + h, 0, 0))],
        out_specs=pl.BlockSpec((tq, width), lambda h, qi: (qi, h)),
        out_shape=jax.ShapeDtypeStruct((t, FOX_HEADS * HEAD_DIM), BF16),
        scratch_shapes=[pltpu.VMEM((hb, tq, LANES), F32), pltpu.VMEM((hb, tq, LANES), F32),
                        pltpu.VMEM((hb, tq, HEAD_DIM), F32)],
        compiler_params=_params("parallel", "arbitrary"),
        name="fox",
    )(q, kt, v, gates_row)


S5_SCAN_LEVELS = 16


def _s5_setup_kernel(are_ref, aim_ref, ls_ref, bre_ref, bim_ref, cre_ref, cim_ref,
                     g_ref, ere_ref, eim_ref, pre_ref, pim_ref, sre_ref, sim_ref, *, gb):
    p, l = S5_GROUP, S5_CHUNK
    lam_re = jnp.minimum(are_ref[...], S5_MAX_RE)
    lam_im = aim_ref[...]
    dt = jnp.exp(ls_ref[...])
    mag = jnp.exp(lam_re * dt)
    a_re = mag * jnp.cos(lam_im * dt)
    a_im = mag * jnp.sin(lam_im * dt)
    nr, ni = a_re - 1.0, a_im
    den = lam_re * lam_re + lam_im * lam_im
    f_re = (nr * lam_re + ni * lam_im) / den
    f_im = (ni * lam_re - nr * lam_im) / den
    b_re, b_im = bre_ref[...], bim_ref[...]
    bb_re = f_re * b_re - f_im * b_im
    bb_im = f_re * b_im + f_im * b_re
    c_re, c_im = cre_ref[...], cim_ref[...]

    pows = [(jnp.ones_like(a_re), jnp.zeros_like(a_im))]
    for _ in range(l):
        pr, pi_ = pows[-1]
        pows.append((pr * a_re - pi_ * a_im, pr * a_im + pi_ * a_re))

    ca_re = [c_re * pr - c_im * pi_ for pr, pi_ in pows]
    ca_im = [c_re * pi_ + c_im * pr for pr, pi_ in pows]
    lo_re = jnp.concatenate(ca_re[:l], axis=1)
    lo_im = jnp.concatenate(ca_im[:l], axis=1)
    for gi in range(gb):
        g_ref[gi] = _dot_nt_f32(lo_re[gi], bb_re[gi]) - _dot_nt_f32(lo_im[gi], bb_im[gi])
    pre_ref[...] = jnp.concatenate(ca_re[1:], axis=1)
    pim_ref[...] = -jnp.concatenate(ca_im[1:], axis=1)
    ere_ref[...] = jnp.concatenate(
        [pows[l - 1 - j][0] * bb_re - pows[l - 1 - j][1] * bb_im for j in range(l)], axis=1)
    eim_ref[...] = jnp.concatenate(
        [pows[l - 1 - j][0] * bb_im + pows[l - 1 - j][1] * bb_re for j in range(l)], axis=1)
    sr, si = pows[l]
    for lv in range(S5_SCAN_LEVELS):
        sre_ref[:, lv:lv + 1, :] = sr
        sim_ref[:, lv:lv + 1, :] = si
        sr, si = sr * sr - si * si, 2.0 * sr * si


def s5_setup(a_re, a_im, log_step, b_re_t, b_im_t, c_re, c_im, gb=8):
    g, n = a_re.shape
    p, l = S5_GROUP, S5_CHUNK
    vec = pl.BlockSpec((gb, 1, n), lambda i: (i, 0, 0))
    mat = pl.BlockSpec((gb, p, n), lambda i: (i, 0, 0))
    big = pl.BlockSpec((gb, l * p, n), lambda i: (i, 0, 0))
    lvl = pl.BlockSpec((gb, S5_SCAN_LEVELS, n), lambda i: (i, 0, 0))
    big_shape = jax.ShapeDtypeStruct((g, l * p, n), F32)
    lvl_shape = jax.ShapeDtypeStruct((g, S5_SCAN_LEVELS, n), F32)
    return pl.pallas_call(
        functools.partial(_s5_setup_kernel, gb=gb),
        grid=(g // gb,),
        in_specs=[vec, vec, pl.BlockSpec((gb, 1, 1), lambda i: (i, 0, 0)), mat, mat, mat, mat],
        out_specs=[pl.BlockSpec((gb, l * p, p), lambda i: (i, 0, 0)), big, big, big, big, lvl, lvl],
        out_shape=[jax.ShapeDtypeStruct((g, l * p, p), F32), big_shape, big_shape, big_shape, big_shape,
                   lvl_shape, lvl_shape],
        compiler_params=_params("parallel"),
        name="s5_setup",
    )(a_re.reshape(g, 1, n), a_im.reshape(g, 1, n), log_step.reshape(g, 1, 1), b_re_t, b_im_t, c_re, c_im)


S5_LANE_GROUPS = LANES // S5_GROUP


def _s5_kernel(u_ref, rk_ref, re_ref, rp_ref, s1_ref, s2_ref, d_ref, o_ref, y_ref, *, levels):
    l, p, gl = S5_CHUNK, S5_GROUP, S5_LANE_GROUPS
    nst = 2 * S5_STATE
    nc = u_ref.shape[0] // l
    u = [u_ref[pl.ds(j, nc, stride=l), :] for j in range(l)]
    u16 = [x.astype(BF16) for x in u]

    def block_diag(rows_per_group, cols_per_group, compressed):
        shape = (gl * rows_per_group, gl * cols_per_group)
        r = lax.broadcasted_iota(jnp.int32, shape, 0) >> int(math.log2(rows_per_group))
        c = lax.broadcasted_iota(jnp.int32, shape, 1) >> int(math.log2(cols_per_group))
        return jnp.where(r == c, jnp.tile(compressed, (gl, 1)), 0.0).astype(BF16)

    e_bd = jnp.concatenate([block_diag(p, nst, re_ref[j]) for j in range(l)], axis=0)
    x = _dot(jnp.concatenate(u16, axis=1), e_bd)
    row = lax.broadcasted_iota(jnp.int32, x.shape, 0)

    def swap_halves(v):
        return jnp.concatenate([pltpu.roll(v[:, g * nst:(g + 1) * nst], S5_STATE, axis=1) for g in range(gl)], axis=1)

    for lv in range(levels):
        sh = 1 << lv
        xs = jnp.where(row >= sh, pltpu.roll(x, sh, axis=0), 0.0)
        x = x + s1_ref[lv:lv + 1, :] * xs + s2_ref[lv:lv + 1, :] * swap_halves(xs)
    s_in = jnp.where(row >= 1, pltpu.roll(x, 1, axis=0), 0.0).astype(BF16)

    k_bd = [block_diag(p, p, rk_ref[k]) for k in range(l)]
    zero_bd = jnp.zeros((LANES, LANES), BF16)
    for j in range(0, l, 2):
        lhs = jnp.concatenate(u16[:j + 2] + [s_in], axis=1)
        rhs_rows = [jnp.concatenate([k_bd[j - i] if i <= j else zero_bd, k_bd[j + 1 - i]], axis=1)
                    for i in range(j + 2)]
        rhs_rows.append(jnp.concatenate([block_diag(nst, p, rp_ref[j]), block_diag(nst, p, rp_ref[j + 1])], axis=1))
        y = _dot(lhs, jnp.concatenate(rhs_rows, axis=0))
        for jj in range(2):
            yj = y[:, jj * LANES:(jj + 1) * LANES] + d_ref[...] * u[j + jj]
            y_ref[pl.ds(j + jj, nc, stride=l), :] = jax.nn.gelu(yj)
    o_ref[...] = y_ref[...].astype(o_ref.dtype)


def s5_scan(hn, rk, re, rp, s1, s2, d_skip):
    t, d = hn.shape
    l, p, gl = S5_CHUNK, S5_GROUP, S5_LANE_GROUPS
    nst = 2 * S5_STATE
    levels = max(1, math.ceil(math.log2(t // l)))
    assert levels <= S5_SCAN_LEVELS

    def per_tile(*shape):
        return pl.BlockSpec((None,) + shape, lambda i: (i,) + (0,) * len(shape))

    col = pl.BlockSpec((t, LANES), lambda i: (0, i))
    return pl.pallas_call(
        functools.partial(_s5_kernel, levels=levels),
        grid=(d // LANES,),
        in_specs=[col, per_tile(l, p, LANES), per_tile(l, p, gl * nst), per_tile(l, nst, LANES),
                  per_tile(S5_SCAN_LEVELS, gl * nst), per_tile(S5_SCAN_LEVELS, gl * nst),
                  pl.BlockSpec((1, LANES), lambda i: (0, i))],
        out_specs=col,
        out_shape=jax.ShapeDtypeStruct((t, d), BF16),
        scratch_shapes=[pltpu.VMEM((t, LANES), F32)],
        compiler_params=_params("parallel"),
        name="s5_scan",
    )(hn, rk, re, rp, s1, s2, d_skip.reshape(1, d))


def s5_mixer(hn, a_re, a_im, log_step, b_re, b_im, c_re, c_im, d_skip):
    t, d = hn.shape
    g, n = a_re.shape
    p, l = S5_GROUP, S5_CHUNK
    nc = t // l
    gm, e_re, e_im, p_re, p_im, s_re, s_im = s5_setup(
        a_re, a_im, log_step, b_re.transpose(0, 2, 1), b_im.transpose(0, 2, 1), c_re, c_im)
    gl, nt, nst = S5_LANE_GROUPS, g // S5_LANE_GROUPS, 2 * n
    rk = gm.reshape(nt, gl, l, p, p).transpose(0, 2, 4, 1, 3).reshape(nt, l, p, gl * p)
    emat = jnp.concatenate([e_re, e_im], axis=-1)
    re = emat.reshape(nt, gl, l, p, nst).transpose(0, 2, 3, 1, 4).reshape(nt, l, p, gl * nst)
    pmat = jnp.concatenate([p_re, p_im], axis=-1)
    rp = pmat.reshape(nt, gl, l, p, nst).transpose(0, 2, 4, 1, 3).reshape(nt, l, nst, gl * p)

    def per_tile(v):
        return v.reshape(nt, gl, S5_SCAN_LEVELS, nst).transpose(0, 2, 1, 3).reshape(nt, S5_SCAN_LEVELS, gl * nst)

    s1 = per_tile(jnp.concatenate([s_re, s_re], axis=-1))
    s2 = per_tile(jnp.concatenate([-s_im, s_im], axis=-1))
    return s5_scan(hn, rk, re, rp, s1, s2, d_skip)


TILES = {"proj": (1024, 512), "gate_up": (1024, 256), "glu": (1024, 256), "down": (512, 256), "out": (1024, 512)}


def _tile(kind, m):
    tm, tn = TILES[kind]
    return dict(tm=min(tm, m), tn=tn)


def _ffn(hn, w_gate, w_up, w_down, layer):
    m = hn.shape[0]
    hidden, w_down16 = matmul_swiglu(hn, Weight(w_gate, layer), Weight(w_up, layer), Weight(w_down, layer),
                                     **_tile("gate_up", m))
    return matmul(hidden, Weight(w_down16), BF16, name="ffn_down", **_tile("down", m))


def kernel(x, norm_g, w_in, gdn_conv_w, gdn_A_log, gdn_dt_bias, gdn_o_norm, fox_f_bias, w_out, s5_A_re, s5_A_im, s5_log_step, s5_B_re, s5_B_im, s5_C_re, s5_C_im, s5_D, s5_w_glu_a, s5_w_glu_b, ffn_w_gate, ffn_w_up, ffn_w_down):
    bsz, t, d = x.shape
    assert bsz == 1
    x = x.reshape(t, d)
    gw = GDN_HEADS * HEAD_DIM
    fw = FOX_HEADS * HEAD_DIM
    proj = _tile("proj", t)

    w0 = w_in[0]
    o_gate = 4 * gw
    o_fox = o_gate + 2 * GDN_HEADS
    o_forget = o_fox + 3 * fw
    w_fq = w0[:, o_fox:o_fox + fw].astype(BF16)
    w_fkt = w0[:, o_fox + fw:o_fox + 2 * fw].T.astype(BF16)
    w_fv = w0[:, o_fox + 2 * fw:o_forget].astype(BF16)
    pad = GATE_LANES - 2 * GDN_HEADS - FOX_HEADS
    w_g = jnp.concatenate([w0[:, o_gate:o_fox], w0[:, o_forget:], jnp.zeros((d, pad), F32)], axis=1).astype(BF16)
    prm = jnp.zeros((SUBLANES, GATE_LANES), F32)
    prm = prm.at[0, :GDN_HEADS].set(gdn_A_log[0]).at[1, :GDN_HEADS].set(gdn_dt_bias[0])
    prm = prm.at[2, 2 * GDN_HEADS:2 * GDN_HEADS + FOX_HEADS].set(fox_f_bias[0])

    hn = rmsnorm(x, norm_g[0, 0], BF16)
    proj_a = matmul(hn, Weight(w_in, 0, n=o_gate), F32, name="proj_gdn", **proj)
    fox_q = matmul(hn, Weight(w_fq), BF16, name="proj_fox_q", scale=FOX_Q_SCALE, **proj)
    fox_kt = matmul_nt(hn, w_fkt, BF16, name="proj_fox_kt", **proj)
    fox_v = matmul(hn, Weight(w_fv), BF16, name="proj_fox_v", **proj)
    gates_col = gates(hn, w_g, prm, rows=min(512, t))
    gates_row = gates_col.T.reshape(GATE_LANES, 1, t)
    o_a = gdn_mixer(proj_a, gdn_conv_w[0], gates_col, gates_row, gdn_o_norm[0])
    o_b = fox_attention(fox_q, fox_kt, fox_v, gates_row, tq=min(512, t))
    m = matmul2(o_a, o_b, Weight(w_out, 0, row_block=0), Weight(w_out, 0, row_block=1), BF16,
                name="out_proj", **_tile("out", t))
    x, hn = resid_norm(x, m, norm_g[0, 1], norm_g[0, 2], BF16)
    f = _ffn(hn, ffn_w_gate, ffn_w_up, ffn_w_down, 0)
    x, hn = resid_norm(x, f, norm_g[0, 3], norm_g[1, 0], F32)

    y = s5_mixer(hn, s5_A_re[0], s5_A_im[0], s5_log_step[0], s5_B_re[0], s5_B_im[0],
                 s5_C_re[0], s5_C_im[0], s5_D[0])
    m = matmul_glu(y, Weight(s5_w_glu_a, 0), Weight(s5_w_glu_b, 0), **_tile("glu", t))
    x, hn = resid_norm(x, m, norm_g[1, 1], norm_g[1, 2], BF16)
    f = _ffn(hn, ffn_w_gate, ffn_w_up, ffn_w_down, 1)
    x = resid_norm(x, f, norm_g[1, 3], None, None)
    return x.reshape(bsz, t, d)
```

```python
import functools
import math
from typing import NamedTuple

import jax
import jax.numpy as jnp
from jax import lax
from jax.experimental import pallas as pl
from jax.experimental.pallas import tpu as pltpu

F32 = jnp.float32
BF16 = jnp.bfloat16

LANES = 128
SUBLANES = 8
VMEM_LIMIT_BYTES = 56 * 1024 * 1024

HEAD_DIM = 128
GDN_HEADS = 16
FOX_HEADS = 16
CONV_WIDTH = 4
GDN_CHUNK = 128
S5_GROUP = 16
S5_STATE = 64
S5_CHUNK = 16
S5_MAX_RE = -1e-4
RMS_EPS = 1e-6
L2_EPS = 1e-6
GATE_LANES = LANES
NEG_INF = float("-inf")
LOG2E = math.log2(math.e)


def _params(*semantics):
    return pltpu.CompilerParams(dimension_semantics=semantics,
                                vmem_limit_bytes=VMEM_LIMIT_BYTES)


def _silu(x):
    return x * jax.nn.sigmoid(x)


def _softplus(x):
    return jnp.maximum(x, 0.0) + jnp.log1p(jnp.exp(-jnp.abs(x)))


def _dot(a, b):
    return jnp.dot(a, b, preferred_element_type=F32)


def _dot_nt(a, b):
    return lax.dot_general(a, b, (((1,), (1,)), ((), ())), preferred_element_type=F32)


def _dot_nt_f32(a, b):
    return lax.dot_general(a, b, (((1,), (1,)), ((), ())), precision=lax.Precision.HIGHEST,
                           preferred_element_type=F32)


def _dot_tn(a, b):
    return lax.dot_general(a, b, (((0,), (0,)), ((), ())), preferred_element_type=F32)


def _split2(x):
    hi = x.astype(BF16)
    lo = (x - hi.astype(F32)).astype(BF16)
    return hi, lo


def _rms(x, g):
    y = x * lax.rsqrt(jnp.mean(x * x, axis=-1, keepdims=True) + RMS_EPS)
    return y * g


def _rmsnorm_kernel(x_ref, g_ref, o_ref):
    o_ref[...] = _rms(x_ref[...], g_ref[...]).astype(o_ref.dtype)


def rmsnorm(x, g, out_dtype, rows=256):
    t, d = x.shape
    return pl.pallas_call(
        _rmsnorm_kernel,
        grid=(t // rows,),
        in_specs=[pl.BlockSpec((rows, d), lambda i: (i, 0)),
                  pl.BlockSpec((1, d), lambda i: (0, 0))],
        out_specs=pl.BlockSpec((rows, d), lambda i: (i, 0)),
        out_shape=jax.ShapeDtypeStruct((t, d), out_dtype),
        compiler_params=_params("parallel"),
        name="rmsnorm",
    )(x, g.reshape(1, d))


def _resid_norm_kernel(x_ref, m_ref, gp_ref, gn_ref, xo_ref, ho_ref):
    xn = x_ref[...] + _rms(m_ref[...].astype(F32), gp_ref[...])
    xo_ref[...] = xn
    ho_ref[...] = _rms(xn, gn_ref[...]).astype(ho_ref.dtype)


def _resid_kernel(x_ref, m_ref, gp_ref, xo_ref):
    xo_ref[...] = x_ref[...] + _rms(m_ref[...].astype(F32), gp_ref[...])


def resid_norm(x, m, g_post, g_next, next_dtype, rows=256):
    t, d = x.shape
    row_spec = pl.BlockSpec((rows, d), lambda i: (i, 0))
    vec_spec = pl.BlockSpec((1, d), lambda i: (0, 0))
    if g_next is None:
        return pl.pallas_call(
            _resid_kernel, grid=(t // rows,),
            in_specs=[row_spec, row_spec, vec_spec], out_specs=row_spec,
            out_shape=jax.ShapeDtypeStruct((t, d), F32),
            compiler_params=_params("parallel"), name="resid",
        )(x, m, g_post.reshape(1, d))
    return pl.pallas_call(
        _resid_norm_kernel, grid=(t // rows,),
        in_specs=[row_spec, row_spec, vec_spec, vec_spec],
        out_specs=[row_spec, row_spec],
        out_shape=[jax.ShapeDtypeStruct((t, d), F32), jax.ShapeDtypeStruct((t, d), next_dtype)],
        compiler_params=_params("parallel"), name="resid_norm",
    )(x, m, g_post.reshape(1, d), g_next.reshape(1, d))


class Weight(NamedTuple):
    array: jax.Array
    layer: int | None = None
    row_block: int = 0
    col0: int = 0
    n: int | None = None

    def cols(self):
        return self.array.shape[-1] if self.n is None else self.n

    def spec(self, k, tn):
        cb0 = self.col0 // tn
        if self.array.ndim == 3:
            return pl.BlockSpec((None, k, tn), lambda i, j: (self.layer, self.row_block, cb0 + j))
        return pl.BlockSpec((k, tn), lambda i, j: (self.row_block, cb0 + j))


def _mm_kernel(a_ref, w_ref, o_ref):
    o_ref[...] = _dot(a_ref[...], w_ref[...].astype(BF16)).astype(o_ref.dtype)


def _mm_scaled_kernel(a_ref, w_ref, o_ref, *, scale):
    o_ref[...] = (_dot(a_ref[...], w_ref[...].astype(BF16)) * scale).astype(o_ref.dtype)


def _mm_swiglu_kernel(a_ref, wg_ref, wu_ref, o_ref):
    a = a_ref[...]
    gate = _dot(a, wg_ref[...].astype(BF16))
    o_ref[...] = (_silu(gate) * _dot(a, wu_ref[...].astype(BF16))).astype(o_ref.dtype)


def _mm_glu_kernel(a_ref, wa_ref, wb_ref, o_ref):
    a = a_ref[...]
    lin = _dot(a, wa_ref[...].astype(BF16))
    o_ref[...] = (lin * jax.nn.sigmoid(_dot(a, wb_ref[...].astype(BF16)))).astype(o_ref.dtype)


def _matmul_call(body, a, weights, out_dtype, tm, tn, name):
    m, k = a.shape
    n = weights[0].cols()
    return pl.pallas_call(
        body,
        grid=(m // tm, n // tn),
        in_specs=[pl.BlockSpec((tm, k), lambda i, j: (i, 0))] + [w.spec(k, tn) for w in weights],
        out_specs=pl.BlockSpec((tm, tn), lambda i, j: (i, j)),
        out_shape=jax.ShapeDtypeStruct((m, n), out_dtype),
        compiler_params=_params("parallel", "arbitrary"),
        name=name,
    )(a, *[w.array for w in weights])


def matmul(a, w, out_dtype, tm, tn, name="matmul", scale=None):
    body = _mm_kernel if scale is None else functools.partial(_mm_scaled_kernel, scale=scale)
    return _matmul_call(body, a, (w,), out_dtype, tm, tn, name)


def _mm2_kernel(a1_ref, a2_ref, w1_ref, w2_ref, o_ref):
    o_ref[...] = (_dot(a1_ref[...], w1_ref[...].astype(BF16))
                  + _dot(a2_ref[...], w2_ref[...].astype(BF16))).astype(o_ref.dtype)


def matmul2(a1, a2, w1, w2, out_dtype, tm, tn, name):
    m, k = a1.shape
    assert a2.shape == a1.shape
    n = w1.cols()
    act = pl.BlockSpec((tm, k), lambda i, j: (i, 0))
    return pl.pallas_call(
        _mm2_kernel,
        grid=(m // tm, n // tn),
        in_specs=[act, act, w1.spec(k, tn), w2.spec(k, tn)],
        out_specs=pl.BlockSpec((tm, tn), lambda i, j: (i, j)),
        out_shape=jax.ShapeDtypeStruct((m, n), out_dtype),
        compiler_params=_params("parallel", "arbitrary"),
        name=name,
    )(a1, a2, w1.array, w2.array)


def _mm_swiglu_cast_kernel(a_ref, wg_ref, wu_ref, wd_ref, o_ref, wd16_ref):
    _mm_swiglu_kernel(a_ref, wg_ref, wu_ref, o_ref)
    wd16_ref[...] = wd_ref[...].astype(BF16)


def matmul_swiglu(a, wg, wu, w_down, tm, tn):
    m, k = a.shape
    n = wg.cols()
    steps_m, steps_n = m // tm, n // tn
    f, d = w_down.array.shape[-2:]
    slab = f // (steps_m * steps_n)
    assert slab * steps_m * steps_n == f and slab % (2 * SUBLANES) == 0
    return pl.pallas_call(
        _mm_swiglu_cast_kernel,
        grid=(steps_m, steps_n),
        in_specs=[pl.BlockSpec((tm, k), lambda i, j: (i, 0)), wg.spec(k, tn), wu.spec(k, tn),
                  pl.BlockSpec((None, slab, d), lambda i, j: (w_down.layer, i * steps_n + j, 0))],
        out_specs=[pl.BlockSpec((tm, tn), lambda i, j: (i, j)),
                   pl.BlockSpec((slab, d), lambda i, j: (i * steps_n + j, 0))],
        out_shape=[jax.ShapeDtypeStruct((m, n), BF16), jax.ShapeDtypeStruct((f, d), BF16)],
        compiler_params=_params("parallel", "arbitrary"),
        name="ffn_gate_up",
    )(a, wg.array, wu.array, w_down.array)


def matmul_glu(a, wa, wb, tm, tn):
    return _matmul_call(_mm_glu_kernel, a, (wa, wb), BF16, tm, tn, "s5_glu")


def _gates_kernel(hn_ref, w_ref, prm_ref, o_ref, carry_ref, *, rows):
    @pl.when(pl.program_id(0) == 0)
    def _():
        carry_ref[...] = jnp.zeros_like(carry_ref)

    p = _dot(hn_ref[...], w_ref[...])
    lane = lax.broadcasted_iota(jnp.int32, (1, GATE_LANES), 1)
    is_decay = lane < GDN_HEADS
    is_beta = (lane >= GDN_HEADS) & (lane < 2 * GDN_HEADS)
    is_forget = (lane >= 2 * GDN_HEADS) & (lane < 2 * GDN_HEADS + FOX_HEADS)
    a_log, dt_bias, f_bias = prm_ref[0:1, :], prm_ref[1:2, :], prm_ref[2:3, :]
    decay = -jnp.exp(a_log) * _softplus(p + dt_bias)
    beta = jax.nn.sigmoid(p)
    log_f = -_softplus(-(p + f_bias))
    vals = jnp.where(is_decay, decay, jnp.where(is_forget, log_f, 0.0))

    r = lax.broadcasted_iota(jnp.int32, (GDN_CHUNK, GDN_CHUNK), 0)
    c = lax.broadcasted_iota(jnp.int32, (GDN_CHUNK, GDN_CHUNK), 1)
    tri = (r >= c).astype(BF16)
    carry = carry_ref[...]
    for ci in range(rows // GDN_CHUNK):
        sl = slice(ci * GDN_CHUNK, (ci + 1) * GDN_CHUNK)
        v = vals[sl]
        hi = v.astype(BF16)
        r1 = v - hi.astype(F32)
        mid = r1.astype(BF16)
        lo = (r1 - mid.astype(F32)).astype(BF16)
        cs = _dot(tri, hi) + _dot(tri, mid) + _dot(tri, lo)
        run = cs + carry
        o_ref[sl, :] = jnp.where(is_decay, cs, jnp.where(is_beta, beta[sl], jnp.where(is_forget, run, 0.0)))
        carry = jnp.where(is_forget, run[GDN_CHUNK - 1:GDN_CHUNK, :], 0.0)
    carry_ref[...] = carry


def gates(hn, w_gates, prm, rows=512):
    t, d = hn.shape
    return pl.pallas_call(
        functools.partial(_gates_kernel, rows=rows),
        grid=(t // rows,),
        in_specs=[pl.BlockSpec((rows, d), lambda i: (i, 0)),
                  pl.BlockSpec((d, GATE_LANES), lambda i: (0, 0)),
                  pl.BlockSpec((SUBLANES, GATE_LANES), lambda i: (0, 0))],
        out_specs=pl.BlockSpec((rows, GATE_LANES), lambda i: (i, 0)),
        out_shape=jax.ShapeDtypeStruct((t, GATE_LANES), F32),
        scratch_shapes=[pltpu.VMEM((1, GATE_LANES), F32)],
        compiler_params=_params("arbitrary"),
        name="gates",
    )(hn, w_gates, prm)


def _dot_split_parts(ah, al, bh, bl):
    return _dot(jnp.concatenate([ah, al], axis=1), jnp.concatenate([bh, bh], axis=0)) + _dot(ah, bl)


def _unit_lower_inverses(lmats):
    n = lmats[0].shape[0]
    r = lax.broadcasted_iota(jnp.int32, (n, n), 0)
    c = lax.broadcasted_iota(jnp.int32, (n, n), 1)
    zero = jnp.zeros((), BF16)
    lsplit = [_split2(l) for l in lmats]
    joins = ((r & 1) == 1) & (c == r - 1)
    invs = [(r == c).astype(F32) - jnp.where(joins, l, 0.0) for l in lmats]
    shift = 1
    while (1 << shift) < n:
        rb = r >> shift
        joins = ((rb & 1) == 1) & ((c >> shift) == rb - 1)
        isplit = [_split2(inv) for inv in invs]
        prods = [_dot_split_parts(jnp.where(joins, lh, zero), jnp.where(joins, ll, zero), ih, il)
                 for (lh, ll), (ih, il) in zip(lsplit, isplit)]
        invs = [inv - _dot_split_parts(ih, il, *_split2(p))
                for inv, (ih, il), p in zip(invs, isplit, prods)]
        shift += 1
    return invs


def _gdn_kernel(q_ref, k_ref, v_ref, z_ref, qh_ref, kh_ref, vh_ref, cwq_ref, cwk_ref, cwv_ref,
                gcol_ref, grow_ref, onorm_ref, o_ref, s_ref, *, rows, hb):
    head0 = pl.program_id(0) * hb
    first = pl.program_id(1) == 0

    @pl.when(first)
    def _():
        s_ref[...] = jnp.zeros_like(s_ref)

    def conv_silu(x_ref, halo_ref, w_ref, cols):
        x = x_ref[:, cols]
        halo = jnp.where(first, 0.0, halo_ref[:, cols])
        xe = jnp.concatenate([halo, x], axis=0)
        w = w_ref[:, cols]
        y = w[CONV_WIDTH - 1:CONV_WIDTH, :] * x
        for i in range(CONV_WIDTH - 1):
            off = SUBLANES - (CONV_WIDTH - 1) + i
            y = y + w[i:i + 1, :] * xe[off:off + rows, :]
        return _silu(y)

    def l2norm(x):
        return x * lax.rsqrt(jnp.sum(x * x, axis=-1, keepdims=True) + L2_EPS)

    n = GDN_CHUNK
    r = lax.broadcasted_iota(jnp.int32, (n, n), 0)
    c = lax.broadcasted_iota(jnp.int32, (n, n), 1)
    lane = lax.broadcasted_iota(jnp.int32, (1, GATE_LANES), 1)
    gates_col = gcol_ref[...]
    chunks = [slice(ci * n, (ci + 1) * n) for ci in range(rows // n)]

    local = []
    for e in range(hb):
        cols = slice(e * HEAD_DIM, (e + 1) * HEAD_DIM)
        q = l2norm(conv_silu(q_ref, qh_ref, cwq_ref, cols)) * (HEAD_DIM ** -0.5)
        k = l2norm(conv_silu(k_ref, kh_ref, cwk_ref, cols))
        v = conv_silu(v_ref, vh_ref, cwv_ref, cols)
        g_col = jnp.sum(jnp.where(lane == head0 + e, gates_col, 0.0), axis=1, keepdims=True)
        beta = jnp.sum(jnp.where(lane == head0 + e + GDN_HEADS, gates_col, 0.0), axis=1, keepdims=True)
        g_row = grow_ref[e]
        for sl in chunks:
            qc, kc, vc, gc, bc = q[sl], k[sl], v[sl], g_col[sl], beta[sl]
            decay = jnp.exp(jnp.where(r >= c, gc - g_row[:, sl], NEG_INF))
            kb = kc * bc
            k16 = kc.astype(BF16)
            qk = _dot_nt(jnp.concatenate([kb, qc], axis=0).astype(BF16), k16) * jnp.tile(decay, (2, 1))
            eg = jnp.exp(gc)
            g_last = gc[n - 1:n, :]
            local.append(dict(
                lmat=jnp.where(r > c, qk[:n], 0.0), attn=qk[n:].astype(BF16),
                rhs=jnp.concatenate([vc * bc, kb * eg], axis=1), q_dec=(qc * eg).astype(BF16),
                k_dec=(kc * jnp.exp(g_last - gc)).astype(BF16), s_dec=jnp.exp(g_last)))
    invs = _unit_lower_inverses([lc["lmat"] for lc in local])
    for lc, inv in zip(local, invs):
        sol = _dot_split_parts(*_split2(inv), *_split2(lc["rhs"]))
        lc["u"], lc["w"] = sol[:, :HEAD_DIM], sol[:, HEAD_DIM:].astype(BF16)

    states = [s_ref[e] for e in range(hb)]
    for ci, sl in enumerate(chunks):
        for e in range(hb):
            lc = local[e * len(chunks) + ci]
            cols = slice(e * HEAD_DIM, (e + 1) * HEAD_DIM)
            s16 = states[e].astype(BF16)
            v_new = lc["u"] - _dot(lc["w"], s16)
            vn16 = v_new.astype(BF16)
            o = _dot(jnp.concatenate([lc["q_dec"], lc["attn"]], axis=1), jnp.concatenate([s16, vn16], axis=0))
            states[e] = states[e] * lc["s_dec"] + _dot_tn(lc["k_dec"], vn16)
            o_ref[sl, cols] = (_rms(o, onorm_ref[...]) * _silu(z_ref[sl, cols])).astype(o_ref.dtype)
    for e in range(hb):
        s_ref[e] = states[e]


def gdn_mixer(proj_a, conv_w, gates_col, gates_row, o_norm, rows=512, hb=2):
    t = proj_a.shape[0]
    nhb = GDN_HEADS // hb
    width = hb * HEAD_DIM
    halo_blocks = rows // SUBLANES

    def col(section):
        return pl.BlockSpec((rows, width), lambda h, i: (i, section * nhb + h))

    def halo(section):
        return pl.BlockSpec((SUBLANES, width),
                            lambda h, i: (jnp.maximum(i * halo_blocks - 1, 0), section * nhb + h))

    def cw(section):
        return pl.BlockSpec((CONV_WIDTH, width), lambda h, i: (0, section * nhb + h))

    return pl.pallas_call(
        functools.partial(_gdn_kernel, rows=rows, hb=hb),
        grid=(nhb, t // rows),
        in_specs=[col(0), col(1), col(2), col(3), halo(0), halo(1), halo(2), cw(0), cw(1), cw(2),
                  pl.BlockSpec((rows, GATE_LANES), lambda h, i: (i, 0)),
                  pl.BlockSpec((hb, 1, rows), lambda h, i: (h, 0, i)),
                  pl.BlockSpec((1, HEAD_DIM), lambda h, i: (0, 0))],
        out_specs=pl.BlockSpec((rows, width), lambda h, i: (i, h)),
        out_shape=jax.ShapeDtypeStruct((t, GDN_HEADS * HEAD_DIM), BF16),
        scratch_shapes=[pltpu.VMEM((hb, HEAD_DIM, HEAD_DIM), F32)],
        compiler_params=_params("parallel", "arbitrary"),
        name="gdn",
    )(proj_a, proj_a, proj_a, proj_a, proj_a, proj_a, proj_a, conv_w, conv_w, conv_w,
      gates_col, gates_row, o_norm.reshape(1, HEAD_DIM))


FOX_Q_SCALE = LOG2E * HEAD_DIM ** -0.5


def _fox_kernel(q_ref, k_ref, v_ref, grow_ref, o_ref, m_ref, l_ref, acc_ref, *, tq, wide, hb):
    qi = pl.program_id(1)
    q0 = pl.multiple_of(qi * tq, tq)
    m_ref[...] = jnp.full_like(m_ref, NEG_INF)
    l_ref[...] = jnp.zeros_like(l_ref)
    acc_ref[...] = jnp.zeros_like(acc_ref)
    heads = [slice(e * HEAD_DIM, (e + 1) * HEAD_DIM) for e in range(hb)]
    cum0 = [grow_ref[e, :, pl.ds(q0, LANES)][:, 0:1] for e in range(hb)]
    q = [q_ref[:, hd] for hd in heads]

    def block(k0, width, diagonal):
        for e, hd in enumerate(heads):
            kj = k_ref[pl.ds(k0, width), hd]
            vj = v_ref[pl.ds(k0, width), hd]
            bias = (cum0[e] - grow_ref[e, :, pl.ds(k0, width)]) * LOG2E
            s = _dot_nt(q[e], kj) + bias
            if diagonal:
                r = lax.broadcasted_iota(jnp.int32, (tq, width), 0)
                c = lax.broadcasted_iota(jnp.int32, (tq, width), 1)
                s = jnp.where(c <= r, s, NEG_INF)
            m_prev = m_ref[e]
            m_new = jnp.maximum(m_prev, jnp.max(s, axis=1, keepdims=True))
            alpha = jnp.exp2(m_prev - m_new)
            p = jnp.exp2(s - jnp.tile(m_new, (1, width // LANES)))
            p_sum = p[:, :LANES]
            for ct in range(1, width // LANES):
                p_sum = p_sum + p[:, ct * LANES:(ct + 1) * LANES]
            l_ref[e] = alpha * l_ref[e] + p_sum
            acc_ref[e] = alpha * acc_ref[e] + _dot(p.astype(BF16), vj)
            m_ref[e] = m_new

    per_wide = wide // tq
    n_wide = qi // per_wide

    def wide_block(j, carry):
        block(pl.multiple_of(j * wide, wide), wide, False)
        return carry

    lax.fori_loop(0, n_wide, wide_block, 0)
    for rem in range(per_wide - 1):
        @pl.when(qi - n_wide * per_wide > rem)
        def _():
            block(pl.multiple_of((n_wide * per_wide + rem) * tq, tq), tq, False)
    block(q0, tq, True)
    for e, hd in enumerate(heads):
        l = jnp.sum(l_ref[e], axis=1, keepdims=True)
        o_ref[:, hd] = (acc_ref[e] / l).astype(o_ref.dtype)


def fox_attention(q, kv, gates_row, tq=512, wide=2048, hb=2):
    t = q.shape[0]
    nhb = FOX_HEADS // hb
    width = hb * HEAD_DIM
    gate_block0 = 2 * GDN_HEADS // hb
    wide = max(tq, min(wide, t))
    return pl.pallas_call(
        functools.partial(_fox_kernel, tq=tq, wide=wide, hb=hb),
        grid=(nhb, t // tq),
        in_specs=[pl.BlockSpec((tq, width), lambda h, qi: (qi, h)),
                  pl.BlockSpec((t, width), lambda h, qi: (0, h)),
                  pl.BlockSpec((t, width), lambda h, qi: (0, nhb + h)),
                  pl.BlockSpec((hb, 1, t), lambda h, qi: (gate_block0 + h, 0, 0))],
        out_specs=pl.BlockSpec((tq, width), lambda h, qi: (qi, h)),
        out_shape=jax.ShapeDtypeStruct((t, FOX_HEADS * HEAD_DIM), BF16),
        scratch_shapes=[pltpu.VMEM((hb, tq, LANES), F32), pltpu.VMEM((hb, tq, LANES), F32),
                        pltpu.VMEM((hb, tq, HEAD_DIM), F32)],
        compiler_params=_params("parallel", "arbitrary"),
        name="fox",
    )(q, kv, kv, gates_row)


S5_SCAN_LEVELS = 16


def _s5_setup_kernel(are_ref, aim_ref, ls_ref, bre_ref, bim_ref, cre_ref, cim_ref,
                     g_ref, ere_ref, eim_ref, pre_ref, pim_ref, sre_ref, sim_ref, *, gb):
    p, l = S5_GROUP, S5_CHUNK
    lam_re = jnp.minimum(are_ref[...], S5_MAX_RE)
    lam_im = aim_ref[...]
    dt = jnp.exp(ls_ref[...])
    mag = jnp.exp(lam_re * dt)
    a_re = mag * jnp.cos(lam_im * dt)
    a_im = mag * jnp.sin(lam_im * dt)
    nr, ni = a_re - 1.0, a_im
    den = lam_re * lam_re + lam_im * lam_im
    f_re = (nr * lam_re + ni * lam_im) / den
    f_im = (ni * lam_re - nr * lam_im) / den
    b_re, b_im = bre_ref[...], bim_ref[...]
    bb_re = f_re * b_re - f_im * b_im
    bb_im = f_re * b_im + f_im * b_re
    c_re, c_im = cre_ref[...], cim_ref[...]

    pows = [(jnp.ones_like(a_re), jnp.zeros_like(a_im))]
    for _ in range(l):
        pr, pi_ = pows[-1]
        pows.append((pr * a_re - pi_ * a_im, pr * a_im + pi_ * a_re))

    ca_re = [c_re * pr - c_im * pi_ for pr, pi_ in pows]
    ca_im = [c_re * pi_ + c_im * pr for pr, pi_ in pows]
    lo_re = jnp.concatenate(ca_re[:l], axis=1)
    lo_im = jnp.concatenate(ca_im[:l], axis=1)
    for gi in range(gb):
        g_ref[gi] = _dot_nt_f32(lo_re[gi], bb_re[gi]) - _dot_nt_f32(lo_im[gi], bb_im[gi])
    pre_ref[...] = jnp.concatenate(ca_re[1:], axis=1)
    pim_ref[...] = -jnp.concatenate(ca_im[1:], axis=1)
    ere_ref[...] = jnp.concatenate(
        [pows[l - 1 - j][0] * bb_re - pows[l - 1 - j][1] * bb_im for j in range(l)], axis=1)
    eim_ref[...] = jnp.concatenate(
        [pows[l - 1 - j][0] * bb_im + pows[l - 1 - j][1] * bb_re for j in range(l)], axis=1)
    sr, si = pows[l]
    for lv in range(S5_SCAN_LEVELS):
        sre_ref[:, lv:lv + 1, :] = sr
        sim_ref[:, lv:lv + 1, :] = si
        sr, si = sr * sr - si * si, 2.0 * sr * si


def s5_setup(a_re, a_im, log_step, b_re_t, b_im_t, c_re, c_im, gb=8):
    g, n = a_re.shape
    p, l = S5_GROUP, S5_CHUNK
    vec = pl.BlockSpec((gb, 1, n), lambda i: (i, 0, 0))
    mat = pl.BlockSpec((gb, p, n), lambda i: (i, 0, 0))
    big = pl.BlockSpec((gb, l * p, n), lambda i: (i, 0, 0))
    lvl = pl.BlockSpec((gb, S5_SCAN_LEVELS, n), lambda i: (i, 0, 0))
    big_shape = jax.ShapeDtypeStruct((g, l * p, n), F32)
    lvl_shape = jax.ShapeDtypeStruct((g, S5_SCAN_LEVELS, n), F32)
    return pl.pallas_call(
        functools.partial(_s5_setup_kernel, gb=gb),
        grid=(g // gb,),
        in_specs=[vec, vec, pl.BlockSpec((gb, 1, 1), lambda i: (i, 0, 0)), mat, mat, mat, mat],
        out_specs=[pl.BlockSpec((gb, l * p, p), lambda i: (i, 0, 0)), big, big, big, big, lvl, lvl],
        out_shape=[jax.ShapeDtypeStruct((g, l * p, p), F32), big_shape, big_shape, big_shape, big_shape,
                   lvl_shape, lvl_shape],
        compiler_params=_params("parallel"),
        name="s5_setup",
    )(a_re.reshape(g, 1, n), a_im.reshape(g, 1, n), log_step.reshape(g, 1, 1), b_re_t, b_im_t, c_re, c_im)


S5_LANE_GROUPS = LANES // S5_GROUP


def _s5_kernel(u_ref, rk_ref, re_ref, rp_ref, s1_ref, s2_ref, d_ref, o_ref, y_ref, *, levels):
    l, p, gl = S5_CHUNK, S5_GROUP, S5_LANE_GROUPS
    nst = 2 * S5_STATE
    nc = u_ref.shape[0] // l
    u = [u_ref[pl.ds(j, nc, stride=l), :] for j in range(l)]
    u16 = [x.astype(BF16) for x in u]

    def block_diag(rows_per_group, cols_per_group, compressed):
        shape = (gl * rows_per_group, gl * cols_per_group)
        r = lax.broadcasted_iota(jnp.int32, shape, 0) >> int(math.log2(rows_per_group))
        c = lax.broadcasted_iota(jnp.int32, shape, 1) >> int(math.log2(cols_per_group))
        return jnp.where(r == c, jnp.tile(compressed, (gl, 1)), 0.0).astype(BF16)

    e_bd = jnp.concatenate([block_diag(p, nst, re_ref[j]) for j in range(l)], axis=0)
    x = _dot(jnp.concatenate(u16, axis=1), e_bd)
    row = lax.broadcasted_iota(jnp.int32, x.shape, 0)

    def swap_halves(v):
        return jnp.concatenate([pltpu.roll(v[:, g * nst:(g + 1) * nst], S5_STATE, axis=1) for g in range(gl)], axis=1)

    for lv in range(levels):
        sh = 1 << lv
        xs = jnp.where(row >= sh, pltpu.roll(x, sh, axis=0), 0.0)
        x = x + s1_ref[lv:lv + 1, :] * xs + s2_ref[lv:lv + 1, :] * swap_halves(xs)
    s_in = jnp.where(row >= 1, pltpu.roll(x, 1, axis=0), 0.0).astype(BF16)

    k_bd = [block_diag(p, p, rk_ref[k]) for k in range(l)]
    zero_bd = jnp.zeros((LANES, LANES), BF16)
    for j in range(0, l, 2):
        lhs = jnp.concatenate(u16[:j + 2] + [s_in], axis=1)
        rhs_rows = [jnp.concatenate([k_bd[j - i] if i <= j else zero_bd, k_bd[j + 1 - i]], axis=1)
                    for i in range(j + 2)]
        rhs_rows.append(jnp.concatenate([block_diag(nst, p, rp_ref[j]), block_diag(nst, p, rp_ref[j + 1])], axis=1))
        y = _dot(lhs, jnp.concatenate(rhs_rows, axis=0))
        for jj in range(2):
            yj = y[:, jj * LANES:(jj + 1) * LANES] + d_ref[...] * u[j + jj]
            y_ref[pl.ds(j + jj, nc, stride=l), :] = jax.nn.gelu(yj)
    o_ref[...] = y_ref[...].astype(o_ref.dtype)


def s5_scan(hn, rk, re, rp, s1, s2, d_skip):
    t, d = hn.shape
    l, p, gl = S5_CHUNK, S5_GROUP, S5_LANE_GROUPS
    nst = 2 * S5_STATE
    levels = max(1, math.ceil(math.log2(t // l)))
    assert levels <= S5_SCAN_LEVELS

    def per_tile(*shape):
        return pl.BlockSpec((None,) + shape, lambda i: (i,) + (0,) * len(shape))

    col = pl.BlockSpec((t, LANES), lambda i: (0, i))
    return pl.pallas_call(
        functools.partial(_s5_kernel, levels=levels),
        grid=(d // LANES,),
        in_specs=[col, per_tile(l, p, LANES), per_tile(l, p, gl * nst), per_tile(l, nst, LANES),
                  per_tile(S5_SCAN_LEVELS, gl * nst), per_tile(S5_SCAN_LEVELS, gl * nst),
                  pl.BlockSpec((1, LANES), lambda i: (0, i))],
        out_specs=col,
        out_shape=jax.ShapeDtypeStruct((t, d), BF16),
        scratch_shapes=[pltpu.VMEM((t, LANES), F32)],
        compiler_params=_params("parallel"),
        name="s5_scan",
    )(hn, rk, re, rp, s1, s2, d_skip.reshape(1, d))


def s5_mixer(hn, a_re, a_im, log_step, b_re, b_im, c_re, c_im, d_skip):
    t, d = hn.shape
    g, n = a_re.shape
    p, l = S5_GROUP, S5_CHUNK
    nc = t // l
    gm, e_re, e_im, p_re, p_im, s_re, s_im = s5_setup(
        a_re, a_im, log_step, b_re.transpose(0, 2, 1), b_im.transpose(0, 2, 1), c_re, c_im)
    gl, nt, nst = S5_LANE_GROUPS, g // S5_LANE_GROUPS, 2 * n
    rk = gm.reshape(nt, gl, l, p, p).transpose(0, 2, 4, 1, 3).reshape(nt, l, p, gl * p)
    emat = jnp.concatenate([e_re, e_im], axis=-1)
    re = emat.reshape(nt, gl, l, p, nst).transpose(0, 2, 3, 1, 4).reshape(nt, l, p, gl * nst)
    pmat = jnp.concatenate([p_re, p_im], axis=-1)
    rp = pmat.reshape(nt, gl, l, p, nst).transpose(0, 2, 4, 1, 3).reshape(nt, l, nst, gl * p)

    def per_tile(v):
        return v.reshape(nt, gl, S5_SCAN_LEVELS, nst).transpose(0, 2, 1, 3).reshape(nt, S5_SCAN_LEVELS, gl * nst)

    s1 = per_tile(jnp.concatenate([s_re, s_re], axis=-1))
    s2 = per_tile(jnp.concatenate([-s_im, s_im], axis=-1))
    return s5_scan(hn, rk, re, rp, s1, s2, d_skip)


TILES = {"proj": (1024, 512), "gate_up": (1024, 256), "glu": (1024, 256), "down": (512, 256), "out": (1024, 512)}


def _tile(kind, m):
    tm, tn = TILES[kind]
    return dict(tm=min(tm, m), tn=tn)


def _ffn(hn, w_gate, w_up, w_down, layer):
    m = hn.shape[0]
    hidden, w_down16 = matmul_swiglu(hn, Weight(w_gate, layer), Weight(w_up, layer), Weight(w_down, layer),
                                     **_tile("gate_up", m))
    return matmul(hidden, Weight(w_down16), BF16, name="ffn_down", **_tile("down", m))


def kernel(x, norm_g, w_in, gdn_conv_w, gdn_A_log, gdn_dt_bias, gdn_o_norm, fox_f_bias, w_out, s5_A_re, s5_A_im, s5_log_step, s5_B_re, s5_B_im, s5_C_re, s5_C_im, s5_D, s5_w_glu_a, s5_w_glu_b, ffn_w_gate, ffn_w_up, ffn_w_down):
    bsz, t, d = x.shape
    assert bsz == 1
    x = x.reshape(t, d)
    gw = GDN_HEADS * HEAD_DIM
    fw = FOX_HEADS * HEAD_DIM
    proj = _tile("proj", t)

    w0 = w_in[0]
    o_gate = 4 * gw
    o_fox = o_gate + 2 * GDN_HEADS
    o_forget = o_fox + 3 * fw
    w_f = w0[:, o_fox:o_forget].astype(BF16)
    pad = GATE_LANES - 2 * GDN_HEADS - FOX_HEADS
    w_g = jnp.concatenate([w0[:, o_gate:o_fox], w0[:, o_forget:], jnp.zeros((d, pad), F32)], axis=1).astype(BF16)
    prm = jnp.zeros((SUBLANES, GATE_LANES), F32)
    prm = prm.at[0, :GDN_HEADS].set(gdn_A_log[0]).at[1, :GDN_HEADS].set(gdn_dt_bias[0])
    prm = prm.at[2, 2 * GDN_HEADS:2 * GDN_HEADS + FOX_HEADS].set(fox_f_bias[0])

    hn = rmsnorm(x, norm_g[0, 0], BF16)
    proj_a = matmul(hn, Weight(w_in, 0, n=o_gate), F32, name="proj_gdn", **proj)
    fox_q = matmul(hn, Weight(w_f, n=fw), BF16, name="proj_fox_q", scale=FOX_Q_SCALE, **proj)
    fox_kv = matmul(hn, Weight(w_f, col0=fw, n=2 * fw), BF16, name="proj_fox_kv", **proj)
    gates_col = gates(hn, w_g, prm, rows=min(512, t))
    gates_row = gates_col.T.reshape(GATE_LANES, 1, t)
    o_a = gdn_mixer(proj_a, gdn_conv_w[0], gates_col, gates_row, gdn_o_norm[0])
    o_b = fox_attention(fox_q, fox_kv, gates_row, tq=min(512, t))
    m = matmul2(o_a, o_b, Weight(w_out, 0, row_block=0), Weight(w_out, 0, row_block=1), BF16,
                name="out_proj", **_tile("out", t))
    x, hn = resid_norm(x, m, norm_g[0, 1], norm_g[0, 2], BF16)
    f = _ffn(hn, ffn_w_gate, ffn_w_up, ffn_w_down, 0)
    x, hn = resid_norm(x, f, norm_g[0, 3], norm_g[1, 0], F32)

    y = s5_mixer(hn, s5_A_re[0], s5_A_im[0], s5_log_step[0], s5_B_re[0], s5_B_im[0],
                 s5_C_re[0], s5_C_im[0], s5_D[0])
    m = matmul_glu(y, Weight(s5_w_glu_a, 0), Weight(s5_w_glu_b, 0), **_tile("glu", t))
    x, hn = resid_norm(x, m, norm_g[1, 1], norm_g[1, 2], BF16)
    f = _ffn(hn, ffn_w_gate, ffn_w_up, ffn_w_down, 1)
    x = resid_norm(x, f, norm_g[1, 3], None, None)
    return x.reshape(bsz, t, d)
```

```python
import functools
import math
from typing import NamedTuple

import jax
import jax.numpy as jnp
from jax import lax
from jax.experimental import pallas as pl
from jax.experimental.pallas import tpu as pltpu

F32 = jnp.float32
BF16 = jnp.bfloat16

LANES = 128
SUBLANES = 8
VMEM_LIMIT_BYTES = 56 * 1024 * 1024

HEAD_DIM = 128
GDN_HEADS = 16
FOX_HEADS = 16
CONV_WIDTH = 4
GDN_CHUNK = 128
S5_GROUP = 16
S5_STATE = 64
S5_CHUNK = 16
S5_MAX_RE = -1e-4
RMS_EPS = 1e-6
L2_EPS = 1e-6
GATE_LANES = LANES
NEG_INF = float("-inf")
LOG2E = math.log2(math.e)


def _params(*semantics):
    return pltpu.CompilerParams(dimension_semantics=semantics,
                                vmem_limit_bytes=VMEM_LIMIT_BYTES)


def _silu(x):
    return x * jax.nn.sigmoid(x)


def _softplus(x):
    return jnp.maximum(x, 0.0) + jnp.log1p(jnp.exp(-jnp.abs(x)))


def _dot(a, b):
    return jnp.dot(a, b, preferred_element_type=F32)


def _dot_nt(a, b):
    return lax.dot_general(a, b, (((1,), (1,)), ((), ())), preferred_element_type=F32)


def _dot_nt_f32(a, b):
    return lax.dot_general(a, b, (((1,), (1,)), ((), ())), precision=lax.Precision.HIGHEST,
                           preferred_element_type=F32)


def _dot_tn(a, b):
    return lax.dot_general(a, b, (((0,), (0,)), ((), ())), preferred_element_type=F32)


def _split2(x):
    hi = x.astype(BF16)
    lo = (x - hi.astype(F32)).astype(BF16)
    return hi, lo


def _rms(x, g):
    y = x * lax.rsqrt(jnp.mean(x * x, axis=-1, keepdims=True) + RMS_EPS)
    return y * g


def _rmsnorm_kernel(x_ref, g_ref, o_ref):
    o_ref[...] = _rms(x_ref[...], g_ref[...]).astype(o_ref.dtype)


def rmsnorm(x, g, out_dtype, rows=256):
    t, d = x.shape
    return pl.pallas_call(
        _rmsnorm_kernel,
        grid=(t // rows,),
        in_specs=[pl.BlockSpec((rows, d), lambda i: (i, 0)),
                  pl.BlockSpec((1, d), lambda i: (0, 0))],
        out_specs=pl.BlockSpec((rows, d), lambda i: (i, 0)),
        out_shape=jax.ShapeDtypeStruct((t, d), out_dtype),
        compiler_params=_params("parallel"),
        name="rmsnorm",
    )(x, g.reshape(1, d))


def _resid_norm_kernel(x_ref, m_ref, gp_ref, gn_ref, xo_ref, ho_ref):
    xn = x_ref[...] + _rms(m_ref[...].astype(F32), gp_ref[...])
    xo_ref[...] = xn
    ho_ref[...] = _rms(xn, gn_ref[...]).astype(ho_ref.dtype)


def _resid_kernel(x_ref, m_ref, gp_ref, xo_ref):
    xo_ref[...] = x_ref[...] + _rms(m_ref[...].astype(F32), gp_ref[...])


def resid_norm(x, m, g_post, g_next, next_dtype, rows=256):
    t, d = x.shape
    row_spec = pl.BlockSpec((rows, d), lambda i: (i, 0))
    vec_spec = pl.BlockSpec((1, d), lambda i: (0, 0))
    if g_next is None:
        return pl.pallas_call(
            _resid_kernel, grid=(t // rows,),
            in_specs=[row_spec, row_spec, vec_spec], out_specs=row_spec,
            out_shape=jax.ShapeDtypeStruct((t, d), F32),
            compiler_params=_params("parallel"), name="resid",
        )(x, m, g_post.reshape(1, d))
    return pl.pallas_call(
        _resid_norm_kernel, grid=(t // rows,),
        in_specs=[row_spec, row_spec, vec_spec, vec_spec],
        out_specs=[row_spec, row_spec],
        out_shape=[jax.ShapeDtypeStruct((t, d), F32), jax.ShapeDtypeStruct((t, d), next_dtype)],
        compiler_params=_params("parallel"), name="resid_norm",
    )(x, m, g_post.reshape(1, d), g_next.reshape(1, d))


class Weight(NamedTuple):
    array: jax.Array
    layer: int | None = None
    row_block: int = 0
    col0: int = 0
    n: int | None = None

    def cols(self):
        return self.array.shape[-1] if self.n is None else self.n

    def spec(self, k, tn):
        cb0 = self.col0 // tn
        if self.array.ndim == 3:
            return pl.BlockSpec((None, k, tn), lambda i, j: (self.layer, self.row_block, cb0 + j))
        return pl.BlockSpec((k, tn), lambda i, j: (self.row_block, cb0 + j))


def _mm_kernel(a_ref, w_ref, o_ref):
    o_ref[...] = _dot(a_ref[...], w_ref[...].astype(BF16)).astype(o_ref.dtype)


def _mm_scaled_kernel(a_ref, w_ref, o_ref, *, scale):
    o_ref[...] = (_dot(a_ref[...], w_ref[...].astype(BF16)) * scale).astype(o_ref.dtype)


def _mm_swiglu_kernel(a_ref, wg_ref, wu_ref, o_ref):
    a = a_ref[...]
    gate = _dot(a, wg_ref[...].astype(BF16))
    o_ref[...] = (_silu(gate) * _dot(a, wu_ref[...].astype(BF16))).astype(o_ref.dtype)


def _mm_glu_kernel(a_ref, wa_ref, wb_ref, o_ref):
    a = a_ref[...]
    lin = _dot(a, wa_ref[...].astype(BF16))
    o_ref[...] = (lin * jax.nn.sigmoid(_dot(a, wb_ref[...].astype(BF16)))).astype(o_ref.dtype)


def _matmul_call(body, a, weights, out_dtype, tm, tn, name):
    m, k = a.shape
    n = weights[0].cols()
    return pl.pallas_call(
        body,
        grid=(m // tm, n // tn),
        in_specs=[pl.BlockSpec((tm, k), lambda i, j: (i, 0))] + [w.spec(k, tn) for w in weights],
        out_specs=pl.BlockSpec((tm, tn), lambda i, j: (i, j)),
        out_shape=jax.ShapeDtypeStruct((m, n), out_dtype),
        compiler_params=_params("parallel", "arbitrary"),
        name=name,
    )(a, *[w.array for w in weights])


def matmul(a, w, out_dtype, tm, tn, name="matmul", scale=None):
    body = _mm_kernel if scale is None else functools.partial(_mm_scaled_kernel, scale=scale)
    return _matmul_call(body, a, (w,), out_dtype, tm, tn, name)


def _mm2_kernel(a1_ref, a2_ref, w1_ref, w2_ref, o_ref):
    o_ref[...] = (_dot(a1_ref[...], w1_ref[...].astype(BF16))
                  + _dot(a2_ref[...], w2_ref[...].astype(BF16))).astype(o_ref.dtype)


def matmul2(a1, a2, w1, w2, out_dtype, tm, tn, name):
    m, k = a1.shape
    assert a2.shape == a1.shape
    n = w1.cols()
    act = pl.BlockSpec((tm, k), lambda i, j: (i, 0))
    return pl.pallas_call(
        _mm2_kernel,
        grid=(m // tm, n // tn),
        in_specs=[act, act, w1.spec(k, tn), w2.spec(k, tn)],
        out_specs=pl.BlockSpec((tm, tn), lambda i, j: (i, j)),
        out_shape=jax.ShapeDtypeStruct((m, n), out_dtype),
        compiler_params=_params("parallel", "arbitrary"),
        name=name,
    )(a1, a2, w1.array, w2.array)


def _mm_swiglu_cast_kernel(a_ref, wg_ref, wu_ref, wd_ref, o_ref, wd16_ref):
    _mm_swiglu_kernel(a_ref, wg_ref, wu_ref, o_ref)
    wd16_ref[...] = wd_ref[...].astype(BF16)


def matmul_swiglu(a, wg, wu, w_down, tm, tn):
    m, k = a.shape
    n = wg.cols()
    steps_m, steps_n = m // tm, n // tn
    f, d = w_down.array.shape[-2:]
    slab = f // (steps_m * steps_n)
    assert slab * steps_m * steps_n == f and slab % (2 * SUBLANES) == 0
    return pl.pallas_call(
        _mm_swiglu_cast_kernel,
        grid=(steps_m, steps_n),
        in_specs=[pl.BlockSpec((tm, k), lambda i, j: (i, 0)), wg.spec(k, tn), wu.spec(k, tn),
                  pl.BlockSpec((None, slab, d), lambda i, j: (w_down.layer, i * steps_n + j, 0))],
        out_specs=[pl.BlockSpec((tm, tn), lambda i, j: (i, j)),
                   pl.BlockSpec((slab, d), lambda i, j: (i * steps_n + j, 0))],
        out_shape=[jax.ShapeDtypeStruct((m, n), BF16), jax.ShapeDtypeStruct((f, d), BF16)],
        compiler_params=_params("parallel", "arbitrary"),
        name="ffn_gate_up",
    )(a, wg.array, wu.array, w_down.array)


def matmul_glu(a, wa, wb, tm, tn):
    return _matmul_call(_mm_glu_kernel, a, (wa, wb), BF16, tm, tn, "s5_glu")


def _gates_kernel(hn_ref, w_ref, prm_ref, o_ref, carry_ref, *, rows):
    @pl.when(pl.program_id(0) == 0)
    def _():
        carry_ref[...] = jnp.zeros_like(carry_ref)

    p = _dot(hn_ref[...], w_ref[...])
    lane = lax.broadcasted_iota(jnp.int32, (1, GATE_LANES), 1)
    is_decay = lane < GDN_HEADS
    is_beta = (lane >= GDN_HEADS) & (lane < 2 * GDN_HEADS)
    is_forget = (lane >= 2 * GDN_HEADS) & (lane < 2 * GDN_HEADS + FOX_HEADS)
    a_log, dt_bias, f_bias = prm_ref[0:1, :], prm_ref[1:2, :], prm_ref[2:3, :]
    decay = -jnp.exp(a_log) * _softplus(p + dt_bias)
    beta = jax.nn.sigmoid(p)
    log_f = -_softplus(-(p + f_bias))
    vals = jnp.where(is_decay, decay, jnp.where(is_forget, log_f, 0.0))

    r = lax.broadcasted_iota(jnp.int32, (GDN_CHUNK, GDN_CHUNK), 0)
    c = lax.broadcasted_iota(jnp.int32, (GDN_CHUNK, GDN_CHUNK), 1)
    tri = (r >= c).astype(BF16)
    carry = carry_ref[...]
    for ci in range(rows // GDN_CHUNK):
        sl = slice(ci * GDN_CHUNK, (ci + 1) * GDN_CHUNK)
        v = vals[sl]
        hi = v.astype(BF16)
        r1 = v - hi.astype(F32)
        mid = r1.astype(BF16)
        lo = (r1 - mid.astype(F32)).astype(BF16)
        cs = _dot(tri, hi) + _dot(tri, mid) + _dot(tri, lo)
        run = cs + carry
        o_ref[sl, :] = jnp.where(is_decay, cs, jnp.where(is_beta, beta[sl], jnp.where(is_forget, run, 0.0)))
        carry = jnp.where(is_forget, run[GDN_CHUNK - 1:GDN_CHUNK, :], 0.0)
    carry_ref[...] = carry


def gates(hn, w_gates, prm, rows=512):
    t, d = hn.shape
    return pl.pallas_call(
        functools.partial(_gates_kernel, rows=rows),
        grid=(t // rows,),
        in_specs=[pl.BlockSpec((rows, d), lambda i: (i, 0)),
                  pl.BlockSpec((d, GATE_LANES), lambda i: (0, 0)),
                  pl.BlockSpec((SUBLANES, GATE_LANES), lambda i: (0, 0))],
        out_specs=pl.BlockSpec((rows, GATE_LANES), lambda i: (i, 0)),
        out_shape=jax.ShapeDtypeStruct((t, GATE_LANES), F32),
        scratch_shapes=[pltpu.VMEM((1, GATE_LANES), F32)],
        compiler_params=_params("arbitrary"),
        name="gates",
    )(hn, w_gates, prm)


def _dot_split_parts(ah, al, bh, bl):
    return _dot(jnp.concatenate([ah, al], axis=1), jnp.concatenate([bh, bh], axis=0)) + _dot(ah, bl)


def _unit_lower_inverses(lmats):
    n = lmats[0].shape[0]
    r = lax.broadcasted_iota(jnp.int32, (n, n), 0)
    c = lax.broadcasted_iota(jnp.int32, (n, n), 1)
    zero = jnp.zeros((), BF16)
    lsplit = [_split2(l) for l in lmats]
    joins = ((r & 1) == 1) & (c == r - 1)
    invs = [(r == c).astype(F32) - jnp.where(joins, l, 0.0) for l in lmats]
    shift = 1
    while (1 << shift) < n:
        rb = r >> shift
        joins = ((rb & 1) == 1) & ((c >> shift) == rb - 1)
        isplit = [_split2(inv) for inv in invs]
        prods = [_dot_split_parts(jnp.where(joins, lh, zero), jnp.where(joins, ll, zero), ih, il)
                 for (lh, ll), (ih, il) in zip(lsplit, isplit)]
        invs = [inv - _dot_split_parts(ih, il, *_split2(p))
                for inv, (ih, il), p in zip(invs, isplit, prods)]
        shift += 1
    return invs


def _gdn_kernel(q_ref, k_ref, v_ref, z_ref, qh_ref, kh_ref, vh_ref, cwq_ref, cwk_ref, cwv_ref,
                gcol_ref, grow_ref, onorm_ref, o_ref, s_ref, *, rows, hb):
    head0 = pl.program_id(0) * hb
    first = pl.program_id(1) == 0

    @pl.when(first)
    def _():
        s_ref[...] = jnp.zeros_like(s_ref)

    def conv_silu(x_ref, halo_ref, w_ref, cols):
        x = x_ref[:, cols]
        halo = jnp.where(first, 0.0, halo_ref[:, cols])
        xe = jnp.concatenate([halo, x], axis=0)
        w = w_ref[:, cols]
        y = w[CONV_WIDTH - 1:CONV_WIDTH, :] * x
        for i in range(CONV_WIDTH - 1):
            off = SUBLANES - (CONV_WIDTH - 1) + i
            y = y + w[i:i + 1, :] * xe[off:off + rows, :]
        return _silu(y)

    def l2norm(x):
        return x * lax.rsqrt(jnp.sum(x * x, axis=-1, keepdims=True) + L2_EPS)

    n = GDN_CHUNK
    r = lax.broadcasted_iota(jnp.int32, (n, n), 0)
    c = lax.broadcasted_iota(jnp.int32, (n, n), 1)
    lane = lax.broadcasted_iota(jnp.int32, (1, GATE_LANES), 1)
    gates_col = gcol_ref[...]
    chunks = [slice(ci * n, (ci + 1) * n) for ci in range(rows // n)]

    local = []
    for e in range(hb):
        cols = slice(e * HEAD_DIM, (e + 1) * HEAD_DIM)
        q = l2norm(conv_silu(q_ref, qh_ref, cwq_ref, cols)) * (HEAD_DIM ** -0.5)
        k = l2norm(conv_silu(k_ref, kh_ref, cwk_ref, cols))
        v = conv_silu(v_ref, vh_ref, cwv_ref, cols)
        g_col = jnp.sum(jnp.where(lane == head0 + e, gates_col, 0.0), axis=1, keepdims=True)
        beta = jnp.sum(jnp.where(lane == head0 + e + GDN_HEADS, gates_col, 0.0), axis=1, keepdims=True)
        g_row = grow_ref[e]
        for sl in chunks:
            qc, kc, vc, gc, bc = q[sl], k[sl], v[sl], g_col[sl], beta[sl]
            decay = jnp.exp(jnp.where(r >= c, gc - g_row[:, sl], NEG_INF))
            kb = kc * bc
            k16 = kc.astype(BF16)
            qk = _dot_nt(jnp.concatenate([kb, qc], axis=0).astype(BF16), k16) * jnp.tile(decay, (2, 1))
            eg = jnp.exp(gc)
            g_last = gc[n - 1:n, :]
            local.append(dict(
                lmat=jnp.where(r > c, qk[:n], 0.0), attn=qk[n:].astype(BF16),
                rhs=jnp.concatenate([vc * bc, kb * eg], axis=1), q_dec=(qc * eg).astype(BF16),
                k_dec=(kc * jnp.exp(g_last - gc)).astype(BF16), s_dec=jnp.exp(g_last)))
    invs = _unit_lower_inverses([lc["lmat"] for lc in local])
    for lc, inv in zip(local, invs):
        sol = _dot_split_parts(*_split2(inv), *_split2(lc["rhs"]))
        lc["u"], lc["w"] = sol[:, :HEAD_DIM], sol[:, HEAD_DIM:].astype(BF16)

    states = [s_ref[e] for e in range(hb)]
    for ci, sl in enumerate(chunks):
        for e in range(hb):
            lc = local[e * len(chunks) + ci]
            cols = slice(e * HEAD_DIM, (e + 1) * HEAD_DIM)
            s16 = states[e].astype(BF16)
            v_new = lc["u"] - _dot(lc["w"], s16)
            vn16 = v_new.astype(BF16)
            o = _dot(jnp.concatenate([lc["q_dec"], lc["attn"]], axis=1), jnp.concatenate([s16, vn16], axis=0))
            states[e] = states[e] * lc["s_dec"] + _dot_tn(lc["k_dec"], vn16)
            o_ref[sl, cols] = (_rms(o, onorm_ref[...]) * _silu(z_ref[sl, cols])).astype(o_ref.dtype)
    for e in range(hb):
        s_ref[e] = states[e]


def gdn_mixer(proj_a, conv_w, gates_col, gates_row, o_norm, rows=512, hb=2):
    t = proj_a.shape[0]
    nhb = GDN_HEADS // hb
    width = hb * HEAD_DIM
    halo_blocks = rows // SUBLANES

    def col(section):
        return pl.BlockSpec((rows, width), lambda h, i: (i, section * nhb + h))

    def halo(section):
        return pl.BlockSpec((SUBLANES, width),
                            lambda h, i: (jnp.maximum(i * halo_blocks - 1, 0), section * nhb + h))

    def cw(section):
        return pl.BlockSpec((CONV_WIDTH, width), lambda h, i: (0, section * nhb + h))

    return pl.pallas_call(
        functools.partial(_gdn_kernel, rows=rows, hb=hb),
        grid=(nhb, t // rows),
        in_specs=[col(0), col(1), col(2), col(3), halo(0), halo(1), halo(2), cw(0), cw(1), cw(2),
                  pl.BlockSpec((rows, GATE_LANES), lambda h, i: (i, 0)),
                  pl.BlockSpec((hb, 1, rows), lambda h, i: (h, 0, i)),
                  pl.BlockSpec((1, HEAD_DIM), lambda h, i: (0, 0))],
        out_specs=pl.BlockSpec((rows, width), lambda h, i: (i, h)),
        out_shape=jax.ShapeDtypeStruct((t, GDN_HEADS * HEAD_DIM), BF16),
        scratch_shapes=[pltpu.VMEM((hb, HEAD_DIM, HEAD_DIM), F32)],
        compiler_params=_params("parallel", "arbitrary"),
        name="gdn",
    )(proj_a, proj_a, proj_a, proj_a, proj_a, proj_a, proj_a, conv_w, conv_w, conv_w,
      gates_col, gates_row, o_norm.reshape(1, HEAD_DIM))


FOX_Q_SCALE = LOG2E * HEAD_DIM ** -0.5


def _fox_kernel(q_ref, k_ref, v_ref, grow_ref, o_ref, m_ref, l_ref, acc_ref, *, tq, wide, hb):
    qi = pl.program_id(1)
    q0 = pl.multiple_of(qi * tq, tq)
    m_ref[...] = jnp.full_like(m_ref, NEG_INF)
    l_ref[...] = jnp.zeros_like(l_ref)
    acc_ref[...] = jnp.zeros_like(acc_ref)
    heads = [slice(e * HEAD_DIM, (e + 1) * HEAD_DIM) for e in range(hb)]
    cum0 = [grow_ref[e, :, pl.ds(q0, LANES)][:, 0:1] for e in range(hb)]
    q = [q_ref[:, hd] for hd in heads]

    def block(k0, width, diagonal):
        for e, hd in enumerate(heads):
            kj = k_ref[pl.ds(k0, width), hd]
            vj = v_ref[pl.ds(k0, width), hd]
            bias = (cum0[e] - grow_ref[e, :, pl.ds(k0, width)]) * LOG2E
            s = _dot_nt(q[e], kj) + bias
            if diagonal:
                r = lax.broadcasted_iota(jnp.int32, (tq, width), 0)
                c = lax.broadcasted_iota(jnp.int32, (tq, width), 1)
                s = jnp.where(c <= r, s, NEG_INF)
            m_prev = m_ref[e]
            m_new = jnp.maximum(m_prev, jnp.max(s, axis=1, keepdims=True))
            alpha = jnp.exp2(m_prev - m_new)
            p = jnp.exp2(s - jnp.tile(m_new, (1, width // LANES)))
            p_sum = p[:, :LANES]
            for ct in range(1, width // LANES):
                p_sum = p_sum + p[:, ct * LANES:(ct + 1) * LANES]
            l_ref[e] = alpha * l_ref[e] + p_sum
            acc_ref[e] = alpha * acc_ref[e] + _dot(p.astype(BF16), vj)
            m_ref[e] = m_new

    per_wide = wide // tq
    n_wide = qi // per_wide

    def wide_block(j, carry):
        block(pl.multiple_of(j * wide, wide), wide, False)
        return carry

    lax.fori_loop(0, n_wide, wide_block, 0)
    for rem in range(per_wide - 1):
        @pl.when(qi - n_wide * per_wide > rem)
        def _():
            block(pl.multiple_of((n_wide * per_wide + rem) * tq, tq), tq, False)
    block(q0, tq, True)
    for e, hd in enumerate(heads):
        l = jnp.sum(l_ref[e], axis=1, keepdims=True)
        o_ref[:, hd] = (acc_ref[e] / l).astype(o_ref.dtype)


def fox_attention(q, kv, gates_row, tq=512, wide=2048, hb=2):
    t = q.shape[0]
    nhb = FOX_HEADS // hb
    width = hb * HEAD_DIM
    gate_block0 = 2 * GDN_HEADS // hb
    wide = max(tq, min(wide, t))
    return pl.pallas_call(
        functools.partial(_fox_kernel, tq=tq, wide=wide, hb=hb),
        grid=(nhb, t // tq),
        in_specs=[pl.BlockSpec((tq, width), lambda h, qi: (qi, h)),
                  pl.BlockSpec((t, width), lambda h, qi: (0, h)),
                  pl.BlockSpec((t, width), lambda h, qi: (0, nhb + h)),
                  pl.BlockSpec((hb, 1, t), lambda h, qi: (gate_block0 + h, 0, 0))],
        out_specs=pl.BlockSpec((tq, width), lambda h, qi: (qi, h)),
        out_shape=jax.ShapeDtypeStruct((t, FOX_HEADS * HEAD_DIM), BF16),
        scratch_shapes=[pltpu.VMEM((hb, tq, LANES), F32), pltpu.VMEM((hb, tq, LANES), F32),
                        pltpu.VMEM((hb, tq, HEAD_DIM), F32)],
        compiler_params=_params("parallel", "arbitrary"),
        name="fox",
    )(q, kv, kv, gates_row)


S5_SCAN_LEVELS = 16


def _s5_setup_kernel(are_ref, aim_ref, ls_ref, bre_ref, bim_ref, cre_ref, cim_ref,
                     g_ref, e_ref, p_ref, s1_ref, s2_ref, *, gb):
    p, l, n = S5_GROUP, S5_CHUNK, S5_STATE
    lam_re = jnp.minimum(are_ref[...], S5_MAX_RE)
    lam_im = aim_ref[...]
    dt = jnp.exp(ls_ref[...])
    mag = jnp.exp(lam_re * dt)
    a_re = mag * jnp.cos(lam_im * dt)
    a_im = mag * jnp.sin(lam_im * dt)
    nr, ni = a_re - 1.0, a_im
    den = lam_re * lam_re + lam_im * lam_im
    f_re = (nr * lam_re + ni * lam_im) / den
    f_im = (ni * lam_re - nr * lam_im) / den
    b_re, b_im = bre_ref[...], bim_ref[...]
    bb_re = f_re * b_re - f_im * b_im
    bb_im = f_re * b_im + f_im * b_re
    c_re, c_im = cre_ref[...], cim_ref[...]

    pows = [(jnp.ones_like(a_re), jnp.zeros_like(a_im))]
    for _ in range(l):
        pr, pi_ = pows[-1]
        pows.append((pr * a_re - pi_ * a_im, pr * a_im + pi_ * a_re))

    ca_re = [c_re * pr - c_im * pi_ for pr, pi_ in pows]
    ca_im = [c_re * pi_ + c_im * pr for pr, pi_ in pows]
    lo_re = jnp.concatenate(ca_re[:l], axis=1)
    lo_im = jnp.concatenate(ca_im[:l], axis=1)
    for gi in range(gb):
        g_ref[gi] = _dot_nt_f32(lo_re[gi], bb_re[gi]) - _dot_nt_f32(lo_im[gi], bb_im[gi])
    p_ref[:, :, :n] = jnp.concatenate(ca_re[1:], axis=1)
    p_ref[:, :, n:] = -jnp.concatenate(ca_im[1:], axis=1)
    e_ref[:, :, :n] = jnp.concatenate(
        [pows[l - 1 - j][0] * bb_re - pows[l - 1 - j][1] * bb_im for j in range(l)], axis=1)
    e_ref[:, :, n:] = jnp.concatenate(
        [pows[l - 1 - j][0] * bb_im + pows[l - 1 - j][1] * bb_re for j in range(l)], axis=1)
    sr, si = pows[l]
    for lv in range(S5_SCAN_LEVELS):
        s1_ref[:, lv:lv + 1, :n] = sr
        s1_ref[:, lv:lv + 1, n:] = sr
        s2_ref[:, lv:lv + 1, :n] = -si
        s2_ref[:, lv:lv + 1, n:] = si
        sr, si = sr * sr - si * si, 2.0 * sr * si


def s5_setup(a_re, a_im, log_step, b_re_t, b_im_t, c_re, c_im, gb=8):
    g, n = a_re.shape
    p, l = S5_GROUP, S5_CHUNK
    vec = pl.BlockSpec((gb, 1, n), lambda i: (i, 0, 0))
    mat = pl.BlockSpec((gb, p, n), lambda i: (i, 0, 0))
    big = pl.BlockSpec((gb, l * p, 2 * n), lambda i: (i, 0, 0))
    lvl = pl.BlockSpec((gb, S5_SCAN_LEVELS, 2 * n), lambda i: (i, 0, 0))
    big_shape = jax.ShapeDtypeStruct((g, l * p, 2 * n), F32)
    lvl_shape = jax.ShapeDtypeStruct((g, S5_SCAN_LEVELS, 2 * n), F32)
    return pl.pallas_call(
        functools.partial(_s5_setup_kernel, gb=gb),
        grid=(g // gb,),
        in_specs=[vec, vec, pl.BlockSpec((gb, 1, 1), lambda i: (i, 0, 0)), mat, mat, mat, mat],
        out_specs=[pl.BlockSpec((gb, l * p, p), lambda i: (i, 0, 0)), big, big, lvl, lvl],
        out_shape=[jax.ShapeDtypeStruct((g, l * p, p), F32), big_shape, big_shape, lvl_shape, lvl_shape],
        compiler_params=_params("parallel"),
        name="s5_setup",
    )(a_re.reshape(g, 1, n), a_im.reshape(g, 1, n), log_step.reshape(g, 1, 1), b_re_t, b_im_t, c_re, c_im)


S5_LANE_GROUPS = LANES // S5_GROUP


def _s5_kernel(u_ref, rk_ref, re_ref, rp_ref, s1_ref, s2_ref, d_ref, o_ref, y_ref, *, levels):
    l, p, gl = S5_CHUNK, S5_GROUP, S5_LANE_GROUPS
    nst = 2 * S5_STATE
    nc = u_ref.shape[0] // l
    u = [u_ref[pl.ds(j, nc, stride=l), :] for j in range(l)]
    u16 = [x.astype(BF16) for x in u]

    def block_diag(rows_per_group, cols_per_group, compressed):
        shape = (gl * rows_per_group, gl * cols_per_group)
        r = lax.broadcasted_iota(jnp.int32, shape, 0) >> int(math.log2(rows_per_group))
        c = lax.broadcasted_iota(jnp.int32, shape, 1) >> int(math.log2(cols_per_group))
        return jnp.where(r == c, jnp.tile(compressed, (gl, 1)), 0.0).astype(BF16)

    e_bd = jnp.concatenate([block_diag(p, nst, re_ref[j]) for j in range(l)], axis=0)
    x = _dot(jnp.concatenate(u16, axis=1), e_bd)
    row = lax.broadcasted_iota(jnp.int32, x.shape, 0)

    def swap_halves(v):
        return jnp.concatenate([pltpu.roll(v[:, g * nst:(g + 1) * nst], S5_STATE, axis=1) for g in range(gl)], axis=1)

    for lv in range(levels):
        sh = 1 << lv
        xs = jnp.where(row >= sh, pltpu.roll(x, sh, axis=0), 0.0)
        x = x + s1_ref[lv:lv + 1, :] * xs + s2_ref[lv:lv + 1, :] * swap_halves(xs)
    s_in = jnp.where(row >= 1, pltpu.roll(x, 1, axis=0), 0.0).astype(BF16)

    k_bd = [block_diag(p, p, rk_ref[k]) for k in range(l)]
    zero_bd = jnp.zeros((LANES, LANES), BF16)
    for j in range(0, l, 2):
        lhs = jnp.concatenate(u16[:j + 2] + [s_in], axis=1)
        rhs_rows = [jnp.concatenate([k_bd[j - i] if i <= j else zero_bd, k_bd[j + 1 - i]], axis=1)
                    for i in range(j + 2)]
        rhs_rows.append(jnp.concatenate([block_diag(nst, p, rp_ref[j]), block_diag(nst, p, rp_ref[j + 1])], axis=1))
        y = _dot(lhs, jnp.concatenate(rhs_rows, axis=0))
        for jj in range(2):
            yj = y[:, jj * LANES:(jj + 1) * LANES] + d_ref[...] * u[j + jj]
            y_ref[pl.ds(j + jj, nc, stride=l), :] = jax.nn.gelu(yj)
    o_ref[...] = y_ref[...].astype(o_ref.dtype)


def s5_scan(hn, rk, re, rp, s1, s2, d_skip):
    t, d = hn.shape
    l, p, gl = S5_CHUNK, S5_GROUP, S5_LANE_GROUPS
    nst = 2 * S5_STATE
    levels = max(1, math.ceil(math.log2(t // l)))
    assert levels <= S5_SCAN_LEVELS

    def per_tile(*shape):
        return pl.BlockSpec((None,) + shape, lambda i: (i,) + (0,) * len(shape))

    col = pl.BlockSpec((t, LANES), lambda i: (0, i))
    return pl.pallas_call(
        functools.partial(_s5_kernel, levels=levels),
        grid=(d // LANES,),
        in_specs=[col, per_tile(l, p, LANES), per_tile(l, p, gl * nst), per_tile(l, nst, LANES),
                  per_tile(S5_SCAN_LEVELS, gl * nst), per_tile(S5_SCAN_LEVELS, gl * nst),
                  pl.BlockSpec((1, LANES), lambda i: (0, i))],
        out_specs=col,
        out_shape=jax.ShapeDtypeStruct((t, d), BF16),
        scratch_shapes=[pltpu.VMEM((t, LANES), F32)],
        compiler_params=_params("parallel"),
        name="s5_scan",
    )(hn, rk, re, rp, s1, s2, d_skip.reshape(1, d))


def s5_mixer(hn, a_re, a_im, log_step, b_re, b_im, c_re, c_im, d_skip):
    g, n = a_re.shape
    p, l = S5_GROUP, S5_CHUNK
    gm, emat, pmat, s1, s2 = s5_setup(
        a_re, a_im, log_step, b_re.transpose(0, 2, 1), b_im.transpose(0, 2, 1), c_re, c_im)
    gl, nt, nst = S5_LANE_GROUPS, g // S5_LANE_GROUPS, 2 * n
    rk = gm.reshape(nt, gl, l, p, p).transpose(0, 2, 4, 1, 3).reshape(nt, l, p, gl * p)
    re = emat.reshape(nt, gl, l, p, nst).transpose(0, 2, 3, 1, 4).reshape(nt, l, p, gl * nst)
    rp = pmat.reshape(nt, gl, l, p, nst).transpose(0, 2, 4, 1, 3).reshape(nt, l, nst, gl * p)

    def per_tile(v):
        return v.reshape(nt, gl, S5_SCAN_LEVELS, nst).transpose(0, 2, 1, 3).reshape(nt, S5_SCAN_LEVELS, gl * nst)

    return s5_scan(hn, rk, re, rp, per_tile(s1), per_tile(s2), d_skip)


TILES = {"proj": (1024, 512), "gate_up": (1024, 256), "glu": (1024, 256), "down": (512, 512), "out": (1024, 512)}


def _tile(kind, m):
    tm, tn = TILES[kind]
    return dict(tm=min(tm, m), tn=tn)


def _ffn(hn, w_gate, w_up, w_down, layer):
    m = hn.shape[0]
    hidden, w_down16 = matmul_swiglu(hn, Weight(w_gate, layer), Weight(w_up, layer), Weight(w_down, layer),
                                     **_tile("gate_up", m))
    return matmul(hidden, Weight(w_down16), BF16, name="ffn_down", **_tile("down", m))


def kernel(x, norm_g, w_in, gdn_conv_w, gdn_A_log, gdn_dt_bias, gdn_o_norm, fox_f_bias, w_out, s5_A_re, s5_A_im, s5_log_step, s5_B_re, s5_B_im, s5_C_re, s5_C_im, s5_D, s5_w_glu_a, s5_w_glu_b, ffn_w_gate, ffn_w_up, ffn_w_down):
    bsz, t, d = x.shape
    assert bsz == 1
    x = x.reshape(t, d)
    gw = GDN_HEADS * HEAD_DIM
    fw = FOX_HEADS * HEAD_DIM
    proj = _tile("proj", t)

    w0 = w_in[0]
    o_gate = 4 * gw
    o_fox = o_gate + 2 * GDN_HEADS
    o_forget = o_fox + 3 * fw
    w_f = w0[:, o_fox:o_forget].astype(BF16)
    pad = GATE_LANES - 2 * GDN_HEADS - FOX_HEADS
    w_g = jnp.concatenate([w0[:, o_gate:o_fox], w0[:, o_forget:], jnp.zeros((d, pad), F32)], axis=1).astype(BF16)
    prm = jnp.zeros((SUBLANES, GATE_LANES), F32)
    prm = prm.at[0, :GDN_HEADS].set(gdn_A_log[0]).at[1, :GDN_HEADS].set(gdn_dt_bias[0])
    prm = prm.at[2, 2 * GDN_HEADS:2 * GDN_HEADS + FOX_HEADS].set(fox_f_bias[0])

    hn = rmsnorm(x, norm_g[0, 0], BF16)
    proj_a = matmul(hn, Weight(w_in, 0, n=o_gate), F32, name="proj_gdn", **proj)
    fox_q = matmul(hn, Weight(w_f, n=fw), BF16, name="proj_fox_q", scale=FOX_Q_SCALE, **proj)
    fox_kv = matmul(hn, Weight(w_f, col0=fw, n=2 * fw), BF16, name="proj_fox_kv", **proj)
    gates_col = gates(hn, w_g, prm, rows=min(512, t))
    gates_row = gates_col.T.reshape(GATE_LANES, 1, t)
    o_a = gdn_mixer(proj_a, gdn_conv_w[0], gates_col, gates_row, gdn_o_norm[0])
    o_b = fox_attention(fox_q, fox_kv, gates_row, tq=min(512, t))
    m = matmul2(o_a, o_b, Weight(w_out, 0, row_block=0), Weight(w_out, 0, row_block=1), BF16,
                name="out_proj", **_tile("out", t))
    x, hn = resid_norm(x, m, norm_g[0, 1], norm_g[0, 2], BF16)
    f = _ffn(hn, ffn_w_gate, ffn_w_up, ffn_w_down, 0)
    x, hn = resid_norm(x, f, norm_g[0, 3], norm_g[1, 0], F32)

    y = s5_mixer(hn, s5_A_re[0], s5_A_im[0], s5_log_step[0], s5_B_re[0], s5_B_im[0],
                 s5_C_re[0], s5_C_im[0], s5_D[0])
    m = matmul_glu(y, Weight(s5_w_glu_a, 0), Weight(s5_w_glu_b, 0), **_tile("glu", t))
    x, hn = resid_norm(x, m, norm_g[1, 1], norm_g[1, 2], BF16)
    f = _ffn(hn, ffn_w_gate, ffn_w_up, ffn_w_down, 1)
    x = resid_norm(x, f, norm_g[1, 3], None, None)
    return x.reshape(bsz, t, d)
```

```python
import functools
import math
from typing import NamedTuple

import jax
import jax.numpy as jnp
from jax import lax
from jax.experimental import pallas as pl
from jax.experimental.pallas import tpu as pltpu

F32 = jnp.float32
BF16 = jnp.bfloat16

LANES = 128
SUBLANES = 8
VMEM_LIMIT_BYTES = 56 * 1024 * 1024

HEAD_DIM = 128
GDN_HEADS = 16
FOX_HEADS = 16
CONV_WIDTH = 4
GDN_CHUNK = 128
S5_GROUP = 16
S5_STATE = 64
S5_CHUNK = 16
S5_MAX_RE = -1e-4
RMS_EPS = 1e-6
L2_EPS = 1e-6
GATE_LANES = LANES
NEG_INF = float("-inf")
LOG2E = math.log2(math.e)


def _params(*semantics):
    return pltpu.CompilerParams(dimension_semantics=semantics,
                                vmem_limit_bytes=VMEM_LIMIT_BYTES)


def _silu(x):
    return x * jax.nn.sigmoid(x)


def _softplus(x):
    return jnp.maximum(x, 0.0) + jnp.log1p(jnp.exp(-jnp.abs(x)))


def _dot(a, b):
    return jnp.dot(a, b, preferred_element_type=F32)


def _dot_nt(a, b):
    return lax.dot_general(a, b, (((1,), (1,)), ((), ())), preferred_element_type=F32)


def _dot_nt_f32(a, b):
    return lax.dot_general(a, b, (((1,), (1,)), ((), ())), precision=lax.Precision.HIGHEST,
                           preferred_element_type=F32)


def _dot_tn(a, b):
    return lax.dot_general(a, b, (((0,), (0,)), ((), ())), preferred_element_type=F32)


def _split2(x):
    hi = x.astype(BF16)
    lo = (x - hi.astype(F32)).astype(BF16)
    return hi, lo


def _rms(x, g):
    y = x * lax.rsqrt(jnp.mean(x * x, axis=-1, keepdims=True) + RMS_EPS)
    return y * g


def _rmsnorm_kernel(x_ref, g_ref, o_ref):
    o_ref[...] = _rms(x_ref[...], g_ref[...]).astype(o_ref.dtype)


def rmsnorm(x, g, out_dtype, rows=256):
    t, d = x.shape
    return pl.pallas_call(
        _rmsnorm_kernel,
        grid=(t // rows,),
        in_specs=[pl.BlockSpec((rows, d), lambda i: (i, 0)),
                  pl.BlockSpec((1, d), lambda i: (0, 0))],
        out_specs=pl.BlockSpec((rows, d), lambda i: (i, 0)),
        out_shape=jax.ShapeDtypeStruct((t, d), out_dtype),
        compiler_params=_params("parallel"),
        name="rmsnorm",
    )(x, g.reshape(1, d))


def _resid_norm_kernel(x_ref, m_ref, gp_ref, gn_ref, xo_ref, ho_ref):
    xn = x_ref[...] + _rms(m_ref[...].astype(F32), gp_ref[...])
    xo_ref[...] = xn
    ho_ref[...] = _rms(xn, gn_ref[...]).astype(ho_ref.dtype)


def _resid_kernel(x_ref, m_ref, gp_ref, xo_ref):
    xo_ref[...] = x_ref[...] + _rms(m_ref[...].astype(F32), gp_ref[...])


def resid_norm(x, m, g_post, g_next, next_dtype, rows=256):
    t, d = x.shape
    row_spec = pl.BlockSpec((rows, d), lambda i: (i, 0))
    vec_spec = pl.BlockSpec((1, d), lambda i: (0, 0))
    if g_next is None:
        return pl.pallas_call(
            _resid_kernel, grid=(t // rows,),
            in_specs=[row_spec, row_spec, vec_spec], out_specs=row_spec,
            out_shape=jax.ShapeDtypeStruct((t, d), F32),
            compiler_params=_params("parallel"), name="resid",
        )(x, m, g_post.reshape(1, d))
    return pl.pallas_call(
        _resid_norm_kernel, grid=(t // rows,),
        in_specs=[row_spec, row_spec, vec_spec, vec_spec],
        out_specs=[row_spec, row_spec],
        out_shape=[jax.ShapeDtypeStruct((t, d), F32), jax.ShapeDtypeStruct((t, d), next_dtype)],
        compiler_params=_params("parallel"), name="resid_norm",
    )(x, m, g_post.reshape(1, d), g_next.reshape(1, d))


class Weight(NamedTuple):
    array: jax.Array
    layer: int | None = None
    row_block: int = 0
    col0: int = 0
    n: int | None = None

    def cols(self):
        return self.array.shape[-1] if self.n is None else self.n

    def spec(self, k, tn):
        cb0 = self.col0 // tn
        if self.array.ndim == 3:
            return pl.BlockSpec((None, k, tn), lambda i, j: (self.layer, self.row_block, cb0 + j))
        return pl.BlockSpec((k, tn), lambda i, j: (self.row_block, cb0 + j))


def _mm_kernel(a_ref, w_ref, o_ref):
    o_ref[...] = _dot(a_ref[...], w_ref[...].astype(BF16)).astype(o_ref.dtype)


def _mm_scaled_kernel(a_ref, w_ref, o_ref, *, scale):
    o_ref[...] = (_dot(a_ref[...], w_ref[...].astype(BF16)) * scale).astype(o_ref.dtype)


def _mm_swiglu_kernel(a_ref, wg_ref, wu_ref, o_ref):
    a = a_ref[...]
    gate = _dot(a, wg_ref[...].astype(BF16))
    o_ref[...] = (_silu(gate) * _dot(a, wu_ref[...].astype(BF16))).astype(o_ref.dtype)


def _mm_glu_kernel(a_ref, wa_ref, wb_ref, o_ref):
    a = a_ref[...]
    lin = _dot(a, wa_ref[...].astype(BF16))
    o_ref[...] = (lin * jax.nn.sigmoid(_dot(a, wb_ref[...].astype(BF16)))).astype(o_ref.dtype)


def _panel_spec(tm, k, single_buffer):
    mode = pl.Buffered(1) if single_buffer else None
    return pl.BlockSpec((tm, k), lambda i, j: (i, 0), pipeline_mode=mode)


def _matmul_call(body, a, weights, out_dtype, tm, tn, name, single_buffer_panel=False):
    m, k = a.shape
    n = weights[0].cols()
    return pl.pallas_call(
        body,
        grid=(m // tm, n // tn),
        in_specs=[_panel_spec(tm, k, single_buffer_panel)] + [w.spec(k, tn) for w in weights],
        out_specs=pl.BlockSpec((tm, tn), lambda i, j: (i, j)),
        out_shape=jax.ShapeDtypeStruct((m, n), out_dtype),
        compiler_params=_params("parallel", "arbitrary"),
        name=name,
    )(a, *[w.array for w in weights])


def matmul(a, w, out_dtype, tm, tn, name="matmul", scale=None):
    body = _mm_kernel if scale is None else functools.partial(_mm_scaled_kernel, scale=scale)
    return _matmul_call(body, a, (w,), out_dtype, tm, tn, name)


def _mm2_kernel(a1_ref, a2_ref, w1_ref, w2_ref, o_ref):
    o_ref[...] = (_dot(a1_ref[...], w1_ref[...].astype(BF16))
                  + _dot(a2_ref[...], w2_ref[...].astype(BF16))).astype(o_ref.dtype)


def matmul2(a1, a2, w1, w2, out_dtype, tm, tn, name):
    m, k = a1.shape
    assert a2.shape == a1.shape
    n = w1.cols()
    act = pl.BlockSpec((tm, k), lambda i, j: (i, 0))
    return pl.pallas_call(
        _mm2_kernel,
        grid=(m // tm, n // tn),
        in_specs=[act, act, w1.spec(k, tn), w2.spec(k, tn)],
        out_specs=pl.BlockSpec((tm, tn), lambda i, j: (i, j)),
        out_shape=jax.ShapeDtypeStruct((m, n), out_dtype),
        compiler_params=_params("parallel", "arbitrary"),
        name=name,
    )(a1, a2, w1.array, w2.array)


def _mm_swiglu_cast_kernel(a_ref, wg_ref, wu_ref, wd_ref, o_ref, wd16_ref):
    _mm_swiglu_kernel(a_ref, wg_ref, wu_ref, o_ref)
    wd16_ref[...] = wd_ref[...].astype(BF16)


def matmul_swiglu(a, wg, wu, w_down, tm, tn):
    m, k = a.shape
    n = wg.cols()
    steps_m, steps_n = m // tm, n // tn
    f, d = w_down.array.shape[-2:]
    slab = f // (steps_m * steps_n)
    assert slab * steps_m * steps_n == f and slab % (2 * SUBLANES) == 0
    return pl.pallas_call(
        _mm_swiglu_cast_kernel,
        grid=(steps_m, steps_n),
        in_specs=[_panel_spec(tm, k, True), wg.spec(k, tn), wu.spec(k, tn),
                  pl.BlockSpec((None, slab, d), lambda i, j: (w_down.layer, i * steps_n + j, 0))],
        out_specs=[pl.BlockSpec((tm, tn), lambda i, j: (i, j)),
                   pl.BlockSpec((slab, d), lambda i, j: (i * steps_n + j, 0))],
        out_shape=[jax.ShapeDtypeStruct((m, n), BF16), jax.ShapeDtypeStruct((f, d), BF16)],
        compiler_params=_params("parallel", "arbitrary"),
        name="ffn_gate_up",
    )(a, wg.array, wu.array, w_down.array)


def matmul_glu(a, wa, wb, tm, tn):
    return _matmul_call(_mm_glu_kernel, a, (wa, wb), BF16, tm, tn, "s5_glu", single_buffer_panel=True)


def _gates_kernel(hn_ref, w_ref, prm_ref, o_ref, carry_ref, *, rows):
    @pl.when(pl.program_id(0) == 0)
    def _():
        carry_ref[...] = jnp.zeros_like(carry_ref)

    p = _dot(hn_ref[...], w_ref[...])
    lane = lax.broadcasted_iota(jnp.int32, (1, GATE_LANES), 1)
    is_decay = lane < GDN_HEADS
    is_beta = (lane >= GDN_HEADS) & (lane < 2 * GDN_HEADS)
    is_forget = (lane >= 2 * GDN_HEADS) & (lane < 2 * GDN_HEADS + FOX_HEADS)
    a_log, dt_bias, f_bias = prm_ref[0:1, :], prm_ref[1:2, :], prm_ref[2:3, :]
    decay = -jnp.exp(a_log) * _softplus(p + dt_bias)
    beta = jax.nn.sigmoid(p)
    log_f = -_softplus(-(p + f_bias))
    vals = jnp.where(is_decay, decay, jnp.where(is_forget, log_f, 0.0))

    r = lax.broadcasted_iota(jnp.int32, (GDN_CHUNK, GDN_CHUNK), 0)
    c = lax.broadcasted_iota(jnp.int32, (GDN_CHUNK, GDN_CHUNK), 1)
    tri = (r >= c).astype(BF16)
    carry = carry_ref[...]
    for ci in range(rows // GDN_CHUNK):
        sl = slice(ci * GDN_CHUNK, (ci + 1) * GDN_CHUNK)
        v = vals[sl]
        hi = v.astype(BF16)
        r1 = v - hi.astype(F32)
        mid = r1.astype(BF16)
        lo = (r1 - mid.astype(F32)).astype(BF16)
        cs = _dot(tri, hi) + _dot(tri, mid) + _dot(tri, lo)
        run = cs + carry
        o_ref[sl, :] = jnp.where(is_decay, cs, jnp.where(is_beta, beta[sl], jnp.where(is_forget, run, 0.0)))
        carry = jnp.where(is_forget, run[GDN_CHUNK - 1:GDN_CHUNK, :], 0.0)
    carry_ref[...] = carry


def gates(hn, w_gates, prm, rows=512):
    t, d = hn.shape
    return pl.pallas_call(
        functools.partial(_gates_kernel, rows=rows),
        grid=(t // rows,),
        in_specs=[pl.BlockSpec((rows, d), lambda i: (i, 0)),
                  pl.BlockSpec((d, GATE_LANES), lambda i: (0, 0)),
                  pl.BlockSpec((SUBLANES, GATE_LANES), lambda i: (0, 0))],
        out_specs=pl.BlockSpec((rows, GATE_LANES), lambda i: (i, 0)),
        out_shape=jax.ShapeDtypeStruct((t, GATE_LANES), F32),
        scratch_shapes=[pltpu.VMEM((1, GATE_LANES), F32)],
        compiler_params=_params("arbitrary"),
        name="gates",
    )(hn, w_gates, prm)


def _dot_split_parts(ah, al, bh, bl):
    return _dot(jnp.concatenate([ah, al], axis=1), jnp.concatenate([bh, bh], axis=0)) + _dot(ah, bl)


def _unit_lower_inverses(lmats):
    n = lmats[0].shape[0]
    r = lax.broadcasted_iota(jnp.int32, (n, n), 0)
    c = lax.broadcasted_iota(jnp.int32, (n, n), 1)
    zero = jnp.zeros((), BF16)
    lsplit = [_split2(l) for l in lmats]
    joins = ((r & 1) == 1) & (c == r - 1)
    invs = [(r == c).astype(F32) - jnp.where(joins, l, 0.0) for l in lmats]
    shift = 1
    while (1 << shift) < n:
        rb = r >> shift
        joins = ((rb & 1) == 1) & ((c >> shift) == rb - 1)
        isplit = [_split2(inv) for inv in invs]
        prods = [_dot_split_parts(jnp.where(joins, lh, zero), jnp.where(joins, ll, zero), ih, il)
                 for (lh, ll), (ih, il) in zip(lsplit, isplit)]
        invs = [inv - _dot_split_parts(ih, il, *_split2(p))
                for inv, (ih, il), p in zip(invs, isplit, prods)]
        shift += 1
    return invs


def _gdn_kernel(q_ref, k_ref, v_ref, z_ref, qh_ref, kh_ref, vh_ref, cwq_ref, cwk_ref, cwv_ref,
                gcol_ref, grow_ref, onorm_ref, o_ref, s_ref, *, rows, hb):
    head0 = pl.program_id(0) * hb
    first = pl.program_id(1) == 0

    @pl.when(first)
    def _():
        s_ref[...] = jnp.zeros_like(s_ref)

    def conv_silu(x_ref, halo_ref, w_ref, cols):
        x = x_ref[:, cols]
        halo = jnp.where(first, 0.0, halo_ref[:, cols])
        xe = jnp.concatenate([halo, x], axis=0)
        w = w_ref[:, cols]
        y = w[CONV_WIDTH - 1:CONV_WIDTH, :] * x
        for i in range(CONV_WIDTH - 1):
            off = SUBLANES - (CONV_WIDTH - 1) + i
            y = y + w[i:i + 1, :] * xe[off:off + rows, :]
        return _silu(y)

    def l2norm(x):
        return x * lax.rsqrt(jnp.sum(x * x, axis=-1, keepdims=True) + L2_EPS)

    n = GDN_CHUNK
    r = lax.broadcasted_iota(jnp.int32, (n, n), 0)
    c = lax.broadcasted_iota(jnp.int32, (n, n), 1)
    lane = lax.broadcasted_iota(jnp.int32, (1, GATE_LANES), 1)
    gates_col = gcol_ref[...]
    chunks = [slice(ci * n, (ci + 1) * n) for ci in range(rows // n)]

    local = []
    for e in range(hb):
        cols = slice(e * HEAD_DIM, (e + 1) * HEAD_DIM)
        q = l2norm(conv_silu(q_ref, qh_ref, cwq_ref, cols)) * (HEAD_DIM ** -0.5)
        k = l2norm(conv_silu(k_ref, kh_ref, cwk_ref, cols))
        v = conv_silu(v_ref, vh_ref, cwv_ref, cols)
        g_col = jnp.sum(jnp.where(lane == head0 + e, gates_col, 0.0), axis=1, keepdims=True)
        beta = jnp.sum(jnp.where(lane == head0 + e + GDN_HEADS, gates_col, 0.0), axis=1, keepdims=True)
        g_row = grow_ref[e]
        for sl in chunks:
            qc, kc, vc, gc, bc = q[sl], k[sl], v[sl], g_col[sl], beta[sl]
            decay = jnp.exp(jnp.where(r >= c, gc - g_row[:, sl], NEG_INF))
            kb = kc * bc
            k16 = kc.astype(BF16)
            qk = _dot_nt(jnp.concatenate([kb, qc], axis=0).astype(BF16), k16) * jnp.tile(decay, (2, 1))
            eg = jnp.exp(gc)
            g_last = gc[n - 1:n, :]
            local.append(dict(
                lmat=jnp.where(r > c, qk[:n], 0.0), attn=qk[n:].astype(BF16),
                rhs=jnp.concatenate([vc * bc, kb * eg], axis=1), q_dec=(qc * eg).astype(BF16),
                k_dec=(kc * jnp.exp(g_last - gc)).astype(BF16), s_dec=jnp.exp(g_last)))
    invs = _unit_lower_inverses([lc["lmat"] for lc in local])
    for lc, inv in zip(local, invs):
        sol = _dot_split_parts(*_split2(inv), *_split2(lc["rhs"]))
        lc["u"], lc["w"] = sol[:, :HEAD_DIM], sol[:, HEAD_DIM:].astype(BF16)

    states = [s_ref[e] for e in range(hb)]
    for ci, sl in enumerate(chunks):
        for e in range(hb):
            lc = local[e * len(chunks) + ci]
            cols = slice(e * HEAD_DIM, (e + 1) * HEAD_DIM)
            s16 = states[e].astype(BF16)
            v_new = lc["u"] - _dot(lc["w"], s16)
            vn16 = v_new.astype(BF16)
            o = _dot(jnp.concatenate([lc["q_dec"], lc["attn"]], axis=1), jnp.concatenate([s16, vn16], axis=0))
            states[e] = states[e] * lc["s_dec"] + _dot_tn(lc["k_dec"], vn16)
            o_ref[sl, cols] = (_rms(o, onorm_ref[...]) * _silu(z_ref[sl, cols])).astype(o_ref.dtype)
    for e in range(hb):
        s_ref[e] = states[e]


def gdn_mixer(proj_a, conv_w, gates_col, gates_row, o_norm, rows=512, hb=2):
    t = proj_a.shape[0]
    nhb = GDN_HEADS // hb
    width = hb * HEAD_DIM
    halo_blocks = rows // SUBLANES

    def col(section):
        return pl.BlockSpec((rows, width), lambda h, i: (i, section * nhb + h))

    def halo(section):
        return pl.BlockSpec((SUBLANES, width),
                            lambda h, i: (jnp.maximum(i * halo_blocks - 1, 0), section * nhb + h))

    def cw(section):
        return pl.BlockSpec((CONV_WIDTH, width), lambda h, i: (0, section * nhb + h))

    return pl.pallas_call(
        functools.partial(_gdn_kernel, rows=rows, hb=hb),
        grid=(nhb, t // rows),
        in_specs=[col(0), col(1), col(2), col(3), halo(0), halo(1), halo(2), cw(0), cw(1), cw(2),
                  pl.BlockSpec((rows, GATE_LANES), lambda h, i: (i, 0)),
                  pl.BlockSpec((hb, 1, rows), lambda h, i: (h, 0, i)),
                  pl.BlockSpec((1, HEAD_DIM), lambda h, i: (0, 0))],
        out_specs=pl.BlockSpec((rows, width), lambda h, i: (i, h)),
        out_shape=jax.ShapeDtypeStruct((t, GDN_HEADS * HEAD_DIM), BF16),
        scratch_shapes=[pltpu.VMEM((hb, HEAD_DIM, HEAD_DIM), F32)],
        compiler_params=_params("parallel", "arbitrary"),
        name="gdn",
    )(proj_a, proj_a, proj_a, proj_a, proj_a, proj_a, proj_a, conv_w, conv_w, conv_w,
      gates_col, gates_row, o_norm.reshape(1, HEAD_DIM))


FOX_Q_SCALE = LOG2E * HEAD_DIM ** -0.5


def _fox_kernel(q_ref, k_ref, v_ref, grow_ref, o_ref, m_ref, l_ref, acc_ref, *, tq, wide, hb):
    qi = pl.program_id(1)
    q0 = pl.multiple_of(qi * tq, tq)
    m_ref[...] = jnp.full_like(m_ref, NEG_INF)
    l_ref[...] = jnp.zeros_like(l_ref)
    acc_ref[...] = jnp.zeros_like(acc_ref)
    heads = [slice(e * HEAD_DIM, (e + 1) * HEAD_DIM) for e in range(hb)]
    cum0 = [grow_ref[e, :, pl.ds(q0, LANES)][:, 0:1] for e in range(hb)]
    q = [q_ref[:, hd] for hd in heads]

    def block(k0, width, diagonal):
        for e, hd in enumerate(heads):
            kj = k_ref[pl.ds(k0, width), hd]
            vj = v_ref[pl.ds(k0, width), hd]
            bias = (cum0[e] - grow_ref[e, :, pl.ds(k0, width)]) * LOG2E
            s = _dot_nt(q[e], kj) + bias
            if diagonal:
                r = lax.broadcasted_iota(jnp.int32, (tq, width), 0)
                c = lax.broadcasted_iota(jnp.int32, (tq, width), 1)
                s = jnp.where(c <= r, s, NEG_INF)
            m_prev = m_ref[e]
            m_new = jnp.maximum(m_prev, jnp.max(s, axis=1, keepdims=True))
            alpha = jnp.exp2(m_prev - m_new)
            p = jnp.exp2(s - jnp.tile(m_new, (1, width // LANES)))
            p_sum = p[:, :LANES]
            for ct in range(1, width // LANES):
                p_sum = p_sum + p[:, ct * LANES:(ct + 1) * LANES]
            l_ref[e] = alpha * l_ref[e] + p_sum
            acc_ref[e] = alpha * acc_ref[e] + _dot(p.astype(BF16), vj)
            m_ref[e] = m_new

    per_wide = wide // tq
    n_wide = qi // per_wide

    def wide_block(j, carry):
        block(pl.multiple_of(j * wide, wide), wide, False)
        return carry

    lax.fori_loop(0, n_wide, wide_block, 0)
    for rem in range(per_wide - 1):
        @pl.when(qi - n_wide * per_wide > rem)
        def _():
            block(pl.multiple_of((n_wide * per_wide + rem) * tq, tq), tq, False)
    block(q0, tq, True)
    for e, hd in enumerate(heads):
        l = jnp.sum(l_ref[e], axis=1, keepdims=True)
        o_ref[:, hd] = (acc_ref[e] / l).astype(o_ref.dtype)


def fox_attention(q, kv, gates_row, tq=512, wide=2048, hb=2):
    t = q.shape[0]
    nhb = FOX_HEADS // hb
    width = hb * HEAD_DIM
    gate_block0 = 2 * GDN_HEADS // hb
    wide = max(tq, min(wide, t))
    return pl.pallas_call(
        functools.partial(_fox_kernel, tq=tq, wide=wide, hb=hb),
        grid=(nhb, t // tq),
        in_specs=[pl.BlockSpec((tq, width), lambda h, qi: (qi, h)),
                  pl.BlockSpec((t, width), lambda h, qi: (0, h)),
                  pl.BlockSpec((t, width), lambda h, qi: (0, nhb + h)),
                  pl.BlockSpec((hb, 1, t), lambda h, qi: (gate_block0 + h, 0, 0))],
        out_specs=pl.BlockSpec((tq, width), lambda h, qi: (qi, h)),
        out_shape=jax.ShapeDtypeStruct((t, FOX_HEADS * HEAD_DIM), BF16),
        scratch_shapes=[pltpu.VMEM((hb, tq, LANES), F32), pltpu.VMEM((hb, tq, LANES), F32),
                        pltpu.VMEM((hb, tq, HEAD_DIM), F32)],
        compiler_params=_params("parallel", "arbitrary"),
        name="fox",
    )(q, kv, kv, gates_row)


S5_SCAN_LEVELS = 16


def _s5_setup_kernel(are_ref, aim_ref, ls_ref, bre_ref, bim_ref, cre_ref, cim_ref,
                     g_ref, e_ref, p_ref, s1_ref, s2_ref, *, gb):
    p, l, n = S5_GROUP, S5_CHUNK, S5_STATE
    lam_re = jnp.minimum(are_ref[...], S5_MAX_RE)
    lam_im = aim_ref[...]
    dt = jnp.exp(ls_ref[...])
    mag = jnp.exp(lam_re * dt)
    a_re = mag * jnp.cos(lam_im * dt)
    a_im = mag * jnp.sin(lam_im * dt)
    nr, ni = a_re - 1.0, a_im
    den = lam_re * lam_re + lam_im * lam_im
    f_re = (nr * lam_re + ni * lam_im) / den
    f_im = (ni * lam_re - nr * lam_im) / den
    b_re, b_im = bre_ref[...], bim_ref[...]
    bb_re = f_re * b_re - f_im * b_im
    bb_im = f_re * b_im + f_im * b_re
    c_re, c_im = cre_ref[...], cim_ref[...]

    pows = [(jnp.ones_like(a_re), jnp.zeros_like(a_im))]
    for _ in range(l):
        pr, pi_ = pows[-1]
        pows.append((pr * a_re - pi_ * a_im, pr * a_im + pi_ * a_re))

    ca_re = [c_re * pr - c_im * pi_ for pr, pi_ in pows]
    ca_im = [c_re * pi_ + c_im * pr for pr, pi_ in pows]
    lo_re = jnp.concatenate(ca_re[:l], axis=1)
    lo_im = jnp.concatenate(ca_im[:l], axis=1)
    for gi in range(gb):
        g_ref[gi] = _dot_nt_f32(lo_re[gi], bb_re[gi]) - _dot_nt_f32(lo_im[gi], bb_im[gi])
    p_ref[:, :, :n] = jnp.concatenate(ca_re[1:], axis=1)
    p_ref[:, :, n:] = -jnp.concatenate(ca_im[1:], axis=1)
    e_ref[:, :, :n] = jnp.concatenate(
        [pows[l - 1 - j][0] * bb_re - pows[l - 1 - j][1] * bb_im for j in range(l)], axis=1)
    e_ref[:, :, n:] = jnp.concatenate(
        [pows[l - 1 - j][0] * bb_im + pows[l - 1 - j][1] * bb_re for j in range(l)], axis=1)
    sr, si = pows[l]
    for lv in range(S5_SCAN_LEVELS):
        s1_ref[:, lv:lv + 1, :n] = sr
        s1_ref[:, lv:lv + 1, n:] = sr
        s2_ref[:, lv:lv + 1, :n] = -si
        s2_ref[:, lv:lv + 1, n:] = si
        sr, si = sr * sr - si * si, 2.0 * sr * si


def s5_setup(a_re, a_im, log_step, b_re_t, b_im_t, c_re, c_im, gb=8):
    g, n = a_re.shape
    p, l = S5_GROUP, S5_CHUNK
    vec = pl.BlockSpec((gb, 1, n), lambda i: (i, 0, 0))
    mat = pl.BlockSpec((gb, p, n), lambda i: (i, 0, 0))
    big = pl.BlockSpec((gb, l * p, 2 * n), lambda i: (i, 0, 0))
    lvl = pl.BlockSpec((gb, S5_SCAN_LEVELS, 2 * n), lambda i: (i, 0, 0))
    big_shape = jax.ShapeDtypeStruct((g, l * p, 2 * n), F32)
    lvl_shape = jax.ShapeDtypeStruct((g, S5_SCAN_LEVELS, 2 * n), F32)
    return pl.pallas_call(
        functools.partial(_s5_setup_kernel, gb=gb),
        grid=(g // gb,),
        in_specs=[vec, vec, pl.BlockSpec((gb, 1, 1), lambda i: (i, 0, 0)), mat, mat, mat, mat],
        out_specs=[pl.BlockSpec((gb, l * p, p), lambda i: (i, 0, 0)), big, big, lvl, lvl],
        out_shape=[jax.ShapeDtypeStruct((g, l * p, p), F32), big_shape, big_shape, lvl_shape, lvl_shape],
        compiler_params=_params("parallel"),
        name="s5_setup",
    )(a_re.reshape(g, 1, n), a_im.reshape(g, 1, n), log_step.reshape(g, 1, 1), b_re_t, b_im_t, c_re, c_im)


S5_LANE_GROUPS = LANES // S5_GROUP


def _s5_kernel(u_ref, rk_ref, re_ref, rp_ref, s1_ref, s2_ref, d_ref, o_ref, y_ref, *, levels):
    l, p, gl = S5_CHUNK, S5_GROUP, S5_LANE_GROUPS
    nst = 2 * S5_STATE
    nc = u_ref.shape[0] // l
    u = [u_ref[pl.ds(j, nc, stride=l), :] for j in range(l)]
    u16 = [x.astype(BF16) for x in u]

    def block_diag(rows_per_group, cols_per_group, compressed):
        shape = (gl * rows_per_group, gl * cols_per_group)
        r = lax.broadcasted_iota(jnp.int32, shape, 0) >> int(math.log2(rows_per_group))
        c = lax.broadcasted_iota(jnp.int32, shape, 1) >> int(math.log2(cols_per_group))
        return jnp.where(r == c, jnp.tile(compressed, (gl, 1)), 0.0).astype(BF16)

    e_bd = jnp.concatenate([block_diag(p, nst, re_ref[j]) for j in range(l)], axis=0)
    x = _dot(jnp.concatenate(u16, axis=1), e_bd)
    row = lax.broadcasted_iota(jnp.int32, x.shape, 0)

    def swap_halves(v):
        return jnp.concatenate([pltpu.roll(v[:, g * nst:(g + 1) * nst], S5_STATE, axis=1) for g in range(gl)], axis=1)

    for lv in range(levels):
        sh = 1 << lv
        xs = jnp.where(row >= sh, pltpu.roll(x, sh, axis=0), 0.0)
        x = x + s1_ref[lv:lv + 1, :] * xs + s2_ref[lv:lv + 1, :] * swap_halves(xs)
    s_in = jnp.where(row >= 1, pltpu.roll(x, 1, axis=0), 0.0).astype(BF16)

    k_bd = [block_diag(p, p, rk_ref[k]) for k in range(l)]
    zero_bd = jnp.zeros((LANES, LANES), BF16)
    for j in range(0, l, 2):
        lhs = jnp.concatenate(u16[:j + 2] + [s_in], axis=1)
        rhs_rows = [jnp.concatenate([k_bd[j - i] if i <= j else zero_bd, k_bd[j + 1 - i]], axis=1)
                    for i in range(j + 2)]
        rhs_rows.append(jnp.concatenate([block_diag(nst, p, rp_ref[j]), block_diag(nst, p, rp_ref[j + 1])], axis=1))
        y = _dot(lhs, jnp.concatenate(rhs_rows, axis=0))
        for jj in range(2):
            yj = y[:, jj * LANES:(jj + 1) * LANES] + d_ref[...] * u[j + jj]
            y_ref[pl.ds(j + jj, nc, stride=l), :] = jax.nn.gelu(yj)
    o_ref[...] = y_ref[...].astype(o_ref.dtype)


def s5_scan(hn, rk, re, rp, s1, s2, d_skip):
    t, d = hn.shape
    l, p, gl = S5_CHUNK, S5_GROUP, S5_LANE_GROUPS
    nst = 2 * S5_STATE
    levels = max(1, math.ceil(math.log2(t // l)))
    assert levels <= S5_SCAN_LEVELS

    def per_tile(*shape):
        return pl.BlockSpec((None,) + shape, lambda i: (i,) + (0,) * len(shape))

    col = pl.BlockSpec((t, LANES), lambda i: (0, i))
    return pl.pallas_call(
        functools.partial(_s5_kernel, levels=levels),
        grid=(d // LANES,),
        in_specs=[col, per_tile(l, p, LANES), per_tile(l, p, gl * nst), per_tile(l, nst, LANES),
                  per_tile(S5_SCAN_LEVELS, gl * nst), per_tile(S5_SCAN_LEVELS, gl * nst),
                  pl.BlockSpec((1, LANES), lambda i: (0, i))],
        out_specs=col,
        out_shape=jax.ShapeDtypeStruct((t, d), BF16),
        scratch_shapes=[pltpu.VMEM((t, LANES), F32)],
        compiler_params=_params("parallel"),
        name="s5_scan",
    )(hn, rk, re, rp, s1, s2, d_skip.reshape(1, d))


def s5_mixer(hn, a_re, a_im, log_step, b_re, b_im, c_re, c_im, d_skip):
    g, n = a_re.shape
    p, l = S5_GROUP, S5_CHUNK
    gm, emat, pmat, s1, s2 = s5_setup(
        a_re, a_im, log_step, b_re.transpose(0, 2, 1), b_im.transpose(0, 2, 1), c_re, c_im)
    gl, nt, nst = S5_LANE_GROUPS, g // S5_LANE_GROUPS, 2 * n
    rk = gm.reshape(nt, gl, l, p, p).transpose(0, 2, 4, 1, 3).reshape(nt, l, p, gl * p)
    re = emat.reshape(nt, gl, l, p, nst).transpose(0, 2, 3, 1, 4).reshape(nt, l, p, gl * nst)
    rp = pmat.reshape(nt, gl, l, p, nst).transpose(0, 2, 4, 1, 3).reshape(nt, l, nst, gl * p)

    def per_tile(v):
        return v.reshape(nt, gl, S5_SCAN_LEVELS, nst).transpose(0, 2, 1, 3).reshape(nt, S5_SCAN_LEVELS, gl * nst)

    return s5_scan(hn, rk, re, rp, per_tile(s1), per_tile(s2), d_skip)


TILES = {"proj": (1024, 512), "gate_up": (2048, 256), "glu": (2048, 256), "down": (512, 512), "out": (1024, 512)}


def _tile(kind, m):
    tm, tn = TILES[kind]
    return dict(tm=min(tm, m), tn=tn)


def _ffn(hn, w_gate, w_up, w_down, layer):
    m = hn.shape[0]
    hidden, w_down16 = matmul_swiglu(hn, Weight(w_gate, layer), Weight(w_up, layer), Weight(w_down, layer),
                                     **_tile("gate_up", m))
    return matmul(hidden, Weight(w_down16), BF16, name="ffn_down", **_tile("down", m))


def kernel(x, norm_g, w_in, gdn_conv_w, gdn_A_log, gdn_dt_bias, gdn_o_norm, fox_f_bias, w_out, s5_A_re, s5_A_im, s5_log_step, s5_B_re, s5_B_im, s5_C_re, s5_C_im, s5_D, s5_w_glu_a, s5_w_glu_b, ffn_w_gate, ffn_w_up, ffn_w_down):
    bsz, t, d = x.shape
    assert bsz == 1
    x = x.reshape(t, d)
    gw = GDN_HEADS * HEAD_DIM
    fw = FOX_HEADS * HEAD_DIM
    proj = _tile("proj", t)

    w0 = w_in[0]
    o_gate = 4 * gw
    o_fox = o_gate + 2 * GDN_HEADS
    o_forget = o_fox + 3 * fw
    w_f = w0[:, o_fox:o_forget].astype(BF16)
    pad = GATE_LANES - 2 * GDN_HEADS - FOX_HEADS
    w_g = jnp.concatenate([w0[:, o_gate:o_fox], w0[:, o_forget:], jnp.zeros((d, pad), F32)], axis=1).astype(BF16)
    prm = jnp.zeros((SUBLANES, GATE_LANES), F32)
    prm = prm.at[0, :GDN_HEADS].set(gdn_A_log[0]).at[1, :GDN_HEADS].set(gdn_dt_bias[0])
    prm = prm.at[2, 2 * GDN_HEADS:2 * GDN_HEADS + FOX_HEADS].set(fox_f_bias[0])

    hn = rmsnorm(x, norm_g[0, 0], BF16)
    proj_a = matmul(hn, Weight(w_in, 0, n=o_gate), F32, name="proj_gdn", **proj)
    fox_q = matmul(hn, Weight(w_f, n=fw), BF16, name="proj_fox_q", scale=FOX_Q_SCALE, **proj)
    fox_kv = matmul(hn, Weight(w_f, col0=fw, n=2 * fw), BF16, name="proj_fox_kv", **proj)
    gates_col = gates(hn, w_g, prm, rows=min(512, t))
    gates_row = gates_col.T.reshape(GATE_LANES, 1, t)
    o_a = gdn_mixer(proj_a, gdn_conv_w[0], gates_col, gates_row, gdn_o_norm[0])
    o_b = fox_attention(fox_q, fox_kv, gates_row, tq=min(512, t))
    m = matmul2(o_a, o_b, Weight(w_out, 0, row_block=0), Weight(w_out, 0, row_block=1), BF16,
                name="out_proj", **_tile("out", t))
    x, hn = resid_norm(x, m, norm_g[0, 1], norm_g[0, 2], BF16)
    f = _ffn(hn, ffn_w_gate, ffn_w_up, ffn_w_down, 0)
    x, hn = resid_norm(x, f, norm_g[0, 3], norm_g[1, 0], F32)

    y = s5_mixer(hn, s5_A_re[0], s5_A_im[0], s5_log_step[0], s5_B_re[0], s5_B_im[0],
                 s5_C_re[0], s5_C_im[0], s5_D[0])
    m = matmul_glu(y, Weight(s5_w_glu_a, 0), Weight(s5_w_glu_b, 0), **_tile("glu", t))
    x, hn = resid_norm(x, m, norm_g[1, 1], norm_g[1, 2], BF16)
    f = _ffn(hn, ffn_w_gate, ffn_w_up, ffn_w_down, 1)
    x = resid_norm(x, f, norm_g[1, 3], None, None)
    return x.reshape(bsz, t, d)
```

```python
import functools
import math
from typing import NamedTuple

import jax
import jax.numpy as jnp
from jax import lax
from jax.experimental import pallas as pl
from jax.experimental.pallas import tpu as pltpu

F32 = jnp.float32
BF16 = jnp.bfloat16

LANES = 128
SUBLANES = 8
VMEM_LIMIT_BYTES = 56 * 1024 * 1024

HEAD_DIM = 128
GDN_HEADS = 16
FOX_HEADS = 16
CONV_WIDTH = 4
GDN_CHUNK = 128
S5_GROUP = 16
S5_STATE = 64
S5_CHUNK = 16
S5_MAX_RE = -1e-4
RMS_EPS = 1e-6
L2_EPS = 1e-6
GATE_LANES = LANES
NEG_INF = float("-inf")
LOG2E = math.log2(math.e)


def _params(*semantics):
    return pltpu.CompilerParams(dimension_semantics=semantics,
                                vmem_limit_bytes=VMEM_LIMIT_BYTES)


def _silu(x):
    return x * jax.nn.sigmoid(x)


def _softplus(x):
    return jnp.maximum(x, 0.0) + jnp.log1p(jnp.exp(-jnp.abs(x)))


def _dot(a, b):
    return jnp.dot(a, b, preferred_element_type=F32)


def _dot_nt(a, b):
    return lax.dot_general(a, b, (((1,), (1,)), ((), ())), preferred_element_type=F32)


def _dot_nt_f32(a, b):
    return lax.dot_general(a, b, (((1,), (1,)), ((), ())), precision=lax.Precision.HIGHEST,
                           preferred_element_type=F32)


def _dot_tn(a, b):
    return lax.dot_general(a, b, (((0,), (0,)), ((), ())), preferred_element_type=F32)


def _split2(x):
    hi = x.astype(BF16)
    lo = (x - hi.astype(F32)).astype(BF16)
    return hi, lo


def _rms(x, g):
    y = x * lax.rsqrt(jnp.mean(x * x, axis=-1, keepdims=True) + RMS_EPS)
    return y * g


def _rmsnorm_kernel(x_ref, g_ref, o_ref):
    o_ref[...] = _rms(x_ref[...], g_ref[...]).astype(o_ref.dtype)


def rmsnorm(x, g, out_dtype, rows=256):
    t, d = x.shape
    return pl.pallas_call(
        _rmsnorm_kernel,
        grid=(t // rows,),
        in_specs=[pl.BlockSpec((rows, d), lambda i: (i, 0)),
                  pl.BlockSpec((1, d), lambda i: (0, 0))],
        out_specs=pl.BlockSpec((rows, d), lambda i: (i, 0)),
        out_shape=jax.ShapeDtypeStruct((t, d), out_dtype),
        compiler_params=_params("parallel"),
        name="rmsnorm",
    )(x, g.reshape(1, d))


def _resid_norm_kernel(x_ref, m_ref, gp_ref, gn_ref, xo_ref, ho_ref):
    xn = x_ref[...] + _rms(m_ref[...].astype(F32), gp_ref[...])
    xo_ref[...] = xn
    ho_ref[...] = _rms(xn, gn_ref[...]).astype(ho_ref.dtype)


def _resid_kernel(x_ref, m_ref, gp_ref, xo_ref):
    xo_ref[...] = x_ref[...] + _rms(m_ref[...].astype(F32), gp_ref[...])


def _resid_scale_kernel(x_ref, m_ref, gp_ref, xo_ref, so_ref):
    xn = x_ref[...] + _rms(m_ref[...].astype(F32), gp_ref[...])
    xo_ref[...] = xn
    inv = lax.rsqrt(jnp.mean(xn * xn, axis=-1, keepdims=True) + RMS_EPS)
    so_ref[...] = jnp.broadcast_to(inv, so_ref.shape)


def resid_scale(x, m, g_post, rows=256):
    t, d = x.shape
    row_spec = pl.BlockSpec((rows, d), lambda i: (i, 0))
    return pl.pallas_call(
        _resid_scale_kernel, grid=(t // rows,),
        in_specs=[row_spec, row_spec, pl.BlockSpec((1, d), lambda i: (0, 0))],
        out_specs=[row_spec, pl.BlockSpec((rows, LANES), lambda i: (i, 0))],
        out_shape=[jax.ShapeDtypeStruct((t, d), F32), jax.ShapeDtypeStruct((t, LANES), F32)],
        compiler_params=_params("parallel"), name="resid_scale",
    )(x, m, g_post.reshape(1, d))


def resid_norm(x, m, g_post, g_next, next_dtype, rows=256):
    t, d = x.shape
    row_spec = pl.BlockSpec((rows, d), lambda i: (i, 0))
    vec_spec = pl.BlockSpec((1, d), lambda i: (0, 0))
    if g_next is None:
        return pl.pallas_call(
            _resid_kernel, grid=(t // rows,),
            in_specs=[row_spec, row_spec, vec_spec], out_specs=row_spec,
            out_shape=jax.ShapeDtypeStruct((t, d), F32),
            compiler_params=_params("parallel"), name="resid",
        )(x, m, g_post.reshape(1, d))
    return pl.pallas_call(
        _resid_norm_kernel, grid=(t // rows,),
        in_specs=[row_spec, row_spec, vec_spec, vec_spec],
        out_specs=[row_spec, row_spec],
        out_shape=[jax.ShapeDtypeStruct((t, d), F32), jax.ShapeDtypeStruct((t, d), next_dtype)],
        compiler_params=_params("parallel"), name="resid_norm",
    )(x, m, g_post.reshape(1, d), g_next.reshape(1, d))


class Weight(NamedTuple):
    array: jax.Array
    layer: int | None = None
    row_block: int = 0
    col0: int = 0
    n: int | None = None

    def cols(self):
        return self.array.shape[-1] if self.n is None else self.n

    def spec(self, k, tn):
        cb0 = self.col0 // tn
        if self.array.ndim == 3:
            return pl.BlockSpec((None, k, tn), lambda i, j: (self.layer, self.row_block, cb0 + j))
        return pl.BlockSpec((k, tn), lambda i, j: (self.row_block, cb0 + j))


def _mm_kernel(a_ref, w_ref, o_ref):
    o_ref[...] = _dot(a_ref[...], w_ref[...].astype(BF16)).astype(o_ref.dtype)


def _mm_scaled_kernel(a_ref, w_ref, o_ref, *, scale):
    o_ref[...] = (_dot(a_ref[...], w_ref[...].astype(BF16)) * scale).astype(o_ref.dtype)


def _mm_swiglu_kernel(a_ref, wg_ref, wu_ref, o_ref):
    a = a_ref[...]
    gate = _dot(a, wg_ref[...].astype(BF16))
    o_ref[...] = (_silu(gate) * _dot(a, wu_ref[...].astype(BF16))).astype(o_ref.dtype)


def _mm_glu_kernel(a_ref, wa_ref, wb_ref, o_ref):
    a = a_ref[...]
    lin = _dot(a, wa_ref[...].astype(BF16))
    o_ref[...] = (lin * jax.nn.sigmoid(_dot(a, wb_ref[...].astype(BF16)))).astype(o_ref.dtype)


def _panel_spec(tm, k, single_buffer):
    mode = pl.Buffered(1) if single_buffer else None
    return pl.BlockSpec((tm, k), lambda i, j: (i, 0), pipeline_mode=mode)


def _matmul_call(body, a, weights, out_dtype, tm, tn, name, single_buffer_panel=False):
    m, k = a.shape
    n = weights[0].cols()
    return pl.pallas_call(
        body,
        grid=(m // tm, n // tn),
        in_specs=[_panel_spec(tm, k, single_buffer_panel)] + [w.spec(k, tn) for w in weights],
        out_specs=pl.BlockSpec((tm, tn), lambda i, j: (i, j)),
        out_shape=jax.ShapeDtypeStruct((m, n), out_dtype),
        compiler_params=_params("parallel", "arbitrary"),
        name=name,
    )(a, *[w.array for w in weights])


def matmul(a, w, out_dtype, tm, tn, name="matmul", scale=None):
    body = _mm_kernel if scale is None else functools.partial(_mm_scaled_kernel, scale=scale)
    return _matmul_call(body, a, (w,), out_dtype, tm, tn, name)


def _mm2_kernel(a1_ref, a2_ref, w1_ref, w2_ref, o_ref):
    o_ref[...] = (_dot(a1_ref[...], w1_ref[...].astype(BF16))
                  + _dot(a2_ref[...], w2_ref[...].astype(BF16))).astype(o_ref.dtype)


def matmul2(a1, a2, w1, w2, out_dtype, tm, tn, name):
    m, k = a1.shape
    assert a2.shape == a1.shape
    n = w1.cols()
    act = pl.BlockSpec((tm, k), lambda i, j: (i, 0))
    return pl.pallas_call(
        _mm2_kernel,
        grid=(m // tm, n // tn),
        in_specs=[act, act, w1.spec(k, tn), w2.spec(k, tn)],
        out_specs=pl.BlockSpec((tm, tn), lambda i, j: (i, j)),
        out_shape=jax.ShapeDtypeStruct((m, n), out_dtype),
        compiler_params=_params("parallel", "arbitrary"),
        name=name,
    )(a1, a2, w1.array, w2.array)


def _mm_swiglu_cast_kernel(a_ref, wg_ref, wu_ref, wd_ref, o_ref, wd16_ref):
    _mm_swiglu_kernel(a_ref, wg_ref, wu_ref, o_ref)
    wd16_ref[...] = wd_ref[...].astype(BF16)


def matmul_swiglu(a, wg, wu, w_down, tm, tn):
    m, k = a.shape
    n = wg.cols()
    steps_m, steps_n = m // tm, n // tn
    f, d = w_down.array.shape[-2:]
    slab = f // (steps_m * steps_n)
    assert slab * steps_m * steps_n == f and slab % (2 * SUBLANES) == 0
    return pl.pallas_call(
        _mm_swiglu_cast_kernel,
        grid=(steps_m, steps_n),
        in_specs=[_panel_spec(tm, k, True), wg.spec(k, tn), wu.spec(k, tn),
                  pl.BlockSpec((None, slab, d), lambda i, j: (w_down.layer, i * steps_n + j, 0))],
        out_specs=[pl.BlockSpec((tm, tn), lambda i, j: (i, j)),
                   pl.BlockSpec((slab, d), lambda i, j: (i * steps_n + j, 0))],
        out_shape=[jax.ShapeDtypeStruct((m, n), BF16), jax.ShapeDtypeStruct((f, d), BF16)],
        compiler_params=_params("parallel", "arbitrary"),
        name="ffn_gate_up",
    )(a, wg.array, wu.array, w_down.array)


def matmul_glu(a, wa, wb, tm, tn):
    return _matmul_call(_mm_glu_kernel, a, (wa, wb), BF16, tm, tn, "s5_glu", single_buffer_panel=True)


def _gates_kernel(hn_ref, w_ref, prm_ref, o_ref, carry_ref, *, rows):
    @pl.when(pl.program_id(0) == 0)
    def _():
        carry_ref[...] = jnp.zeros_like(carry_ref)

    p = _dot(hn_ref[...], w_ref[...])
    lane = lax.broadcasted_iota(jnp.int32, (1, GATE_LANES), 1)
    is_decay = lane < GDN_HEADS
    is_beta = (lane >= GDN_HEADS) & (lane < 2 * GDN_HEADS)
    is_forget = (lane >= 2 * GDN_HEADS) & (lane < 2 * GDN_HEADS + FOX_HEADS)
    a_log, dt_bias, f_bias = prm_ref[0:1, :], prm_ref[1:2, :], prm_ref[2:3, :]
    decay = -jnp.exp(a_log) * _softplus(p + dt_bias)
    beta = jax.nn.sigmoid(p)
    log_f = -_softplus(-(p + f_bias))
    vals = jnp.where(is_decay, decay, jnp.where(is_forget, log_f, 0.0))

    r = lax.broadcasted_iota(jnp.int32, (GDN_CHUNK, GDN_CHUNK), 0)
    c = lax.broadcasted_iota(jnp.int32, (GDN_CHUNK, GDN_CHUNK), 1)
    tri = (r >= c).astype(BF16)
    carry = carry_ref[...]
    for ci in range(rows // GDN_CHUNK):
        sl = slice(ci * GDN_CHUNK, (ci + 1) * GDN_CHUNK)
        v = vals[sl]
        hi = v.astype(BF16)
        r1 = v - hi.astype(F32)
        mid = r1.astype(BF16)
        lo = (r1 - mid.astype(F32)).astype(BF16)
        cs = _dot(tri, hi) + _dot(tri, mid) + _dot(tri, lo)
        run = cs + carry
        o_ref[sl, :] = jnp.where(is_decay, cs, jnp.where(is_beta, beta[sl], jnp.where(is_forget, run, 0.0)))
        carry = jnp.where(is_forget, run[GDN_CHUNK - 1:GDN_CHUNK, :], 0.0)
    carry_ref[...] = carry


def gates(hn, w_gates, prm, rows=512):
    t, d = hn.shape
    return pl.pallas_call(
        functools.partial(_gates_kernel, rows=rows),
        grid=(t // rows,),
        in_specs=[pl.BlockSpec((rows, d), lambda i: (i, 0)),
                  pl.BlockSpec((d, GATE_LANES), lambda i: (0, 0)),
                  pl.BlockSpec((SUBLANES, GATE_LANES), lambda i: (0, 0))],
        out_specs=pl.BlockSpec((rows, GATE_LANES), lambda i: (i, 0)),
        out_shape=jax.ShapeDtypeStruct((t, GATE_LANES), F32),
        scratch_shapes=[pltpu.VMEM((1, GATE_LANES), F32)],
        compiler_params=_params("arbitrary"),
        name="gates",
    )(hn, w_gates, prm)


def _dot_split_parts(ah, al, bh, bl):
    return _dot(jnp.concatenate([ah, al], axis=1), jnp.concatenate([bh, bh], axis=0)) + _dot(ah, bl)


def _unit_lower_inverses(lmats):
    n = lmats[0].shape[0]
    r = lax.broadcasted_iota(jnp.int32, (n, n), 0)
    c = lax.broadcasted_iota(jnp.int32, (n, n), 1)
    zero = jnp.zeros((), BF16)
    lsplit = [_split2(l) for l in lmats]
    joins = ((r & 1) == 1) & (c == r - 1)
    invs = [(r == c).astype(F32) - jnp.where(joins, l, 0.0) for l in lmats]
    shift = 1
    while (1 << shift) < n:
        rb = r >> shift
        joins = ((rb & 1) == 1) & ((c >> shift) == rb - 1)
        isplit = [_split2(inv) for inv in invs]
        prods = [_dot_split_parts(jnp.where(joins, lh, zero), jnp.where(joins, ll, zero), ih, il)
                 for (lh, ll), (ih, il) in zip(lsplit, isplit)]
        invs = [inv - _dot_split_parts(ih, il, *_split2(p))
                for inv, (ih, il), p in zip(invs, isplit, prods)]
        shift += 1
    return invs


def _gdn_kernel(q_ref, k_ref, v_ref, z_ref, qh_ref, kh_ref, vh_ref, cwq_ref, cwk_ref, cwv_ref,
                gcol_ref, grow_ref, onorm_ref, o_ref, s_ref, *, rows, hb):
    head0 = pl.program_id(0) * hb
    first = pl.program_id(1) == 0

    @pl.when(first)
    def _():
        s_ref[...] = jnp.zeros_like(s_ref)

    def conv_silu(x_ref, halo_ref, w_ref, cols):
        x = x_ref[:, cols]
        halo = jnp.where(first, 0.0, halo_ref[:, cols])
        xe = jnp.concatenate([halo, x], axis=0)
        w = w_ref[:, cols]
        y = w[CONV_WIDTH - 1:CONV_WIDTH, :] * x
        for i in range(CONV_WIDTH - 1):
            off = SUBLANES - (CONV_WIDTH - 1) + i
            y = y + w[i:i + 1, :] * xe[off:off + rows, :]
        return _silu(y)

    def l2norm(x):
        return x * lax.rsqrt(jnp.sum(x * x, axis=-1, keepdims=True) + L2_EPS)

    n = GDN_CHUNK
    r = lax.broadcasted_iota(jnp.int32, (n, n), 0)
    c = lax.broadcasted_iota(jnp.int32, (n, n), 1)
    lane = lax.broadcasted_iota(jnp.int32, (1, GATE_LANES), 1)
    gates_col = gcol_ref[...]
    chunks = [slice(ci * n, (ci + 1) * n) for ci in range(rows // n)]

    local = []
    for e in range(hb):
        cols = slice(e * HEAD_DIM, (e + 1) * HEAD_DIM)
        q = l2norm(conv_silu(q_ref, qh_ref, cwq_ref, cols)) * (HEAD_DIM ** -0.5)
        k = l2norm(conv_silu(k_ref, kh_ref, cwk_ref, cols))
        v = conv_silu(v_ref, vh_ref, cwv_ref, cols)
        g_col = jnp.sum(jnp.where(lane == head0 + e, gates_col, 0.0), axis=1, keepdims=True)
        beta = jnp.sum(jnp.where(lane == head0 + e + GDN_HEADS, gates_col, 0.0), axis=1, keepdims=True)
        g_row = grow_ref[e]
        for sl in chunks:
            qc, kc, vc, gc, bc = q[sl], k[sl], v[sl], g_col[sl], beta[sl]
            decay = jnp.exp(jnp.where(r >= c, gc - g_row[:, sl], NEG_INF))
            kb = kc * bc
            k16 = kc.astype(BF16)
            qk = _dot_nt(jnp.concatenate([kb, qc], axis=0).astype(BF16), k16) * jnp.tile(decay, (2, 1))
            eg = jnp.exp(gc)
            g_last = gc[n - 1:n, :]
            local.append(dict(
                lmat=jnp.where(r > c, qk[:n], 0.0), attn=qk[n:].astype(BF16),
                rhs=jnp.concatenate([vc * bc, kb * eg], axis=1), q_dec=(qc * eg).astype(BF16),
                k_dec=(kc * jnp.exp(g_last - gc)).astype(BF16), s_dec=jnp.exp(g_last)))
    invs = _unit_lower_inverses([lc["lmat"] for lc in local])
    for lc, inv in zip(local, invs):
        sol = _dot_split_parts(*_split2(inv), *_split2(lc["rhs"]))
        lc["u"], lc["w"] = sol[:, :HEAD_DIM], sol[:, HEAD_DIM:].astype(BF16)

    states = [s_ref[e] for e in range(hb)]
    for ci, sl in enumerate(chunks):
        for e in range(hb):
            lc = local[e * len(chunks) + ci]
            cols = slice(e * HEAD_DIM, (e + 1) * HEAD_DIM)
            s16 = states[e].astype(BF16)
            v_new = lc["u"] - _dot(lc["w"], s16)
            vn16 = v_new.astype(BF16)
            o = _dot(jnp.concatenate([lc["q_dec"], lc["attn"]], axis=1), jnp.concatenate([s16, vn16], axis=0))
            states[e] = states[e] * lc["s_dec"] + _dot_tn(lc["k_dec"], vn16)
            o_ref[sl, cols] = (_rms(o, onorm_ref[...]) * _silu(z_ref[sl, cols])).astype(o_ref.dtype)
    for e in range(hb):
        s_ref[e] = states[e]


def gdn_mixer(proj_a, conv_w, gates_col, gates_row, o_norm, rows=512, hb=2):
    t = proj_a.shape[0]
    nhb = GDN_HEADS // hb
    width = hb * HEAD_DIM
    halo_blocks = rows // SUBLANES

    def col(section):
        return pl.BlockSpec((rows, width), lambda h, i: (i, section * nhb + h))

    def halo(section):
        return pl.BlockSpec((SUBLANES, width),
                            lambda h, i: (jnp.maximum(i * halo_blocks - 1, 0), section * nhb + h))

    def cw(section):
        return pl.BlockSpec((CONV_WIDTH, width), lambda h, i: (0, section * nhb + h))

    return pl.pallas_call(
        functools.partial(_gdn_kernel, rows=rows, hb=hb),
        grid=(nhb, t // rows),
        in_specs=[col(0), col(1), col(2), col(3), halo(0), halo(1), halo(2), cw(0), cw(1), cw(2),
                  pl.BlockSpec((rows, GATE_LANES), lambda h, i: (i, 0)),
                  pl.BlockSpec((hb, 1, rows), lambda h, i: (h, 0, i)),
                  pl.BlockSpec((1, HEAD_DIM), lambda h, i: (0, 0))],
        out_specs=pl.BlockSpec((rows, width), lambda h, i: (i, h)),
        out_shape=jax.ShapeDtypeStruct((t, GDN_HEADS * HEAD_DIM), BF16),
        scratch_shapes=[pltpu.VMEM((hb, HEAD_DIM, HEAD_DIM), F32)],
        compiler_params=_params("parallel", "arbitrary"),
        name="gdn",
    )(proj_a, proj_a, proj_a, proj_a, proj_a, proj_a, proj_a, conv_w, conv_w, conv_w,
      gates_col, gates_row, o_norm.reshape(1, HEAD_DIM))


FOX_Q_SCALE = LOG2E * HEAD_DIM ** -0.5


def _fox_kernel(q_ref, k_ref, v_ref, grow_ref, o_ref, m_ref, l_ref, acc_ref, *, tq, wide, hb):
    qi = pl.program_id(1)
    q0 = pl.multiple_of(qi * tq, tq)
    m_ref[...] = jnp.full_like(m_ref, NEG_INF)
    l_ref[...] = jnp.zeros_like(l_ref)
    acc_ref[...] = jnp.zeros_like(acc_ref)
    heads = [slice(e * HEAD_DIM, (e + 1) * HEAD_DIM) for e in range(hb)]
    cum0 = [grow_ref[e, :, pl.ds(q0, LANES)][:, 0:1] for e in range(hb)]
    q = [q_ref[:, hd] for hd in heads]

    def block(k0, width, diagonal):
        for e, hd in enumerate(heads):
            kj = k_ref[pl.ds(k0, width), hd]
            vj = v_ref[pl.ds(k0, width), hd]
            bias = (cum0[e] - grow_ref[e, :, pl.ds(k0, width)]) * LOG2E
            s = _dot_nt(q[e], kj) + bias
            if diagonal:
                r = lax.broadcasted_iota(jnp.int32, (tq, width), 0)
                c = lax.broadcasted_iota(jnp.int32, (tq, width), 1)
                s = jnp.where(c <= r, s, NEG_INF)
            m_prev = m_ref[e]
            m_new = jnp.maximum(m_prev, jnp.max(s, axis=1, keepdims=True))
            alpha = jnp.exp2(m_prev - m_new)
            p = jnp.exp2(s - jnp.tile(m_new, (1, width // LANES)))
            p_sum = p[:, :LANES]
            for ct in range(1, width // LANES):
                p_sum = p_sum + p[:, ct * LANES:(ct + 1) * LANES]
            l_ref[e] = alpha * l_ref[e] + p_sum
            acc_ref[e] = alpha * acc_ref[e] + _dot(p.astype(BF16), vj)
            m_ref[e] = m_new

    per_wide = wide // tq
    n_wide = qi // per_wide

    def wide_block(j, carry):
        block(pl.multiple_of(j * wide, wide), wide, False)
        return carry

    lax.fori_loop(0, n_wide, wide_block, 0)
    for rem in range(per_wide - 1):
        @pl.when(qi - n_wide * per_wide > rem)
        def _():
            block(pl.multiple_of((n_wide * per_wide + rem) * tq, tq), tq, False)
    block(q0, tq, True)
    for e, hd in enumerate(heads):
        l = jnp.sum(l_ref[e], axis=1, keepdims=True)
        o_ref[:, hd] = (acc_ref[e] / l).astype(o_ref.dtype)


def fox_attention(q, kv, gates_row, tq=512, wide=2048, hb=2):
    t = q.shape[0]
    nhb = FOX_HEADS // hb
    width = hb * HEAD_DIM
    gate_block0 = 2 * GDN_HEADS // hb
    wide = max(tq, min(wide, t))
    return pl.pallas_call(
        functools.partial(_fox_kernel, tq=tq, wide=wide, hb=hb),
        grid=(nhb, t // tq),
        in_specs=[pl.BlockSpec((tq, width), lambda h, qi: (qi, h)),
                  pl.BlockSpec((t, width), lambda h, qi: (0, h)),
                  pl.BlockSpec((t, width), lambda h, qi: (0, nhb + h)),
                  pl.BlockSpec((hb, 1, t), lambda h, qi: (gate_block0 + h, 0, 0))],
        out_specs=pl.BlockSpec((tq, width), lambda h, qi: (qi, h)),
        out_shape=jax.ShapeDtypeStruct((t, FOX_HEADS * HEAD_DIM), BF16),
        scratch_shapes=[pltpu.VMEM((hb, tq, LANES), F32), pltpu.VMEM((hb, tq, LANES), F32),
                        pltpu.VMEM((hb, tq, HEAD_DIM), F32)],
        compiler_params=_params("parallel", "arbitrary"),
        name="fox",
    )(q, kv, kv, gates_row)


S5_SCAN_LEVELS = 16


def _s5_setup_kernel(are_ref, aim_ref, ls_ref, bre_ref, bim_ref, cre_ref, cim_ref,
                     g_ref, e_ref, p_ref, s1_ref, s2_ref, *, gb):
    p, l, n = S5_GROUP, S5_CHUNK, S5_STATE
    lam_re = jnp.minimum(are_ref[...], S5_MAX_RE)
    lam_im = aim_ref[...]
    dt = jnp.exp(ls_ref[...])
    mag = jnp.exp(lam_re * dt)
    a_re = mag * jnp.cos(lam_im * dt)
    a_im = mag * jnp.sin(lam_im * dt)
    nr, ni = a_re - 1.0, a_im
    den = lam_re * lam_re + lam_im * lam_im
    f_re = (nr * lam_re + ni * lam_im) / den
    f_im = (ni * lam_re - nr * lam_im) / den
    b_re, b_im = bre_ref[...], bim_ref[...]
    bb_re = f_re * b_re - f_im * b_im
    bb_im = f_re * b_im + f_im * b_re
    c_re, c_im = cre_ref[...], cim_ref[...]

    pows = [(jnp.ones_like(a_re), jnp.zeros_like(a_im))]
    for _ in range(l):
        pr, pi_ = pows[-1]
        pows.append((pr * a_re - pi_ * a_im, pr * a_im + pi_ * a_re))

    ca_re = [c_re * pr - c_im * pi_ for pr, pi_ in pows]
    ca_im = [c_re * pi_ + c_im * pr for pr, pi_ in pows]
    lo_re = jnp.concatenate(ca_re[:l], axis=1)
    lo_im = jnp.concatenate(ca_im[:l], axis=1)
    for gi in range(gb):
        g_ref[gi] = _dot_nt_f32(lo_re[gi], bb_re[gi]) - _dot_nt_f32(lo_im[gi], bb_im[gi])
    p_ref[:, :, :n] = jnp.concatenate(ca_re[1:], axis=1)
    p_ref[:, :, n:] = -jnp.concatenate(ca_im[1:], axis=1)
    e_ref[:, :, :n] = jnp.concatenate(
        [pows[l - 1 - j][0] * bb_re - pows[l - 1 - j][1] * bb_im for j in range(l)], axis=1)
    e_ref[:, :, n:] = jnp.concatenate(
        [pows[l - 1 - j][0] * bb_im + pows[l - 1 - j][1] * bb_re for j in range(l)], axis=1)
    sr, si = pows[l]
    for lv in range(S5_SCAN_LEVELS):
        s1_ref[:, lv:lv + 1, :n] = sr
        s1_ref[:, lv:lv + 1, n:] = sr
        s2_ref[:, lv:lv + 1, :n] = -si
        s2_ref[:, lv:lv + 1, n:] = si
        sr, si = sr * sr - si * si, 2.0 * sr * si


def s5_setup(a_re, a_im, log_step, b_re_t, b_im_t, c_re, c_im, gb=8):
    g, n = a_re.shape
    p, l = S5_GROUP, S5_CHUNK
    vec = pl.BlockSpec((gb, 1, n), lambda i: (i, 0, 0))
    mat = pl.BlockSpec((gb, p, n), lambda i: (i, 0, 0))
    big = pl.BlockSpec((gb, l * p, 2 * n), lambda i: (i, 0, 0))
    lvl = pl.BlockSpec((gb, S5_SCAN_LEVELS, 2 * n), lambda i: (i, 0, 0))
    big_shape = jax.ShapeDtypeStruct((g, l * p, 2 * n), F32)
    lvl_shape = jax.ShapeDtypeStruct((g, S5_SCAN_LEVELS, 2 * n), F32)
    return pl.pallas_call(
        functools.partial(_s5_setup_kernel, gb=gb),
        grid=(g // gb,),
        in_specs=[vec, vec, pl.BlockSpec((gb, 1, 1), lambda i: (i, 0, 0)), mat, mat, mat, mat],
        out_specs=[pl.BlockSpec((gb, l * p, p), lambda i: (i, 0, 0)), big, big, lvl, lvl],
        out_shape=[jax.ShapeDtypeStruct((g, l * p, p), F32), big_shape, big_shape, lvl_shape, lvl_shape],
        compiler_params=_params("parallel"),
        name="s5_setup",
    )(a_re.reshape(g, 1, n), a_im.reshape(g, 1, n), log_step.reshape(g, 1, 1), b_re_t, b_im_t, c_re, c_im)


S5_LANE_GROUPS = LANES // S5_GROUP


def _s5_kernel(x_ref, inv_ref, gain_ref, rk_ref, re_ref, rp_ref, s1_ref, s2_ref, d_ref, o_ref, y_ref, *, levels):
    l, p, gl = S5_CHUNK, S5_GROUP, S5_LANE_GROUPS
    nst = 2 * S5_STATE
    nc = x_ref.shape[0] // l
    u = [x_ref[pl.ds(j, nc, stride=l), :] * inv_ref[pl.ds(j, nc, stride=l), :] * gain_ref[...] for j in range(l)]
    u16 = [x.astype(BF16) for x in u]

    def block_diag(rows_per_group, cols_per_group, compressed):
        shape = (gl * rows_per_group, gl * cols_per_group)
        r = lax.broadcasted_iota(jnp.int32, shape, 0) >> int(math.log2(rows_per_group))
        c = lax.broadcasted_iota(jnp.int32, shape, 1) >> int(math.log2(cols_per_group))
        return jnp.where(r == c, jnp.tile(compressed, (gl, 1)), 0.0).astype(BF16)

    e_bd = jnp.concatenate([block_diag(p, nst, re_ref[j]) for j in range(l)], axis=0)
    x = _dot(jnp.concatenate(u16, axis=1), e_bd)
    row = lax.broadcasted_iota(jnp.int32, x.shape, 0)

    def swap_halves(v):
        return jnp.concatenate([pltpu.roll(v[:, g * nst:(g + 1) * nst], S5_STATE, axis=1) for g in range(gl)], axis=1)

    for lv in range(levels):
        sh = 1 << lv
        xs = jnp.where(row >= sh, pltpu.roll(x, sh, axis=0), 0.0)
        x = x + s1_ref[lv:lv + 1, :] * xs + s2_ref[lv:lv + 1, :] * swap_halves(xs)
    s_in = jnp.where(row >= 1, pltpu.roll(x, 1, axis=0), 0.0).astype(BF16)

    k_bd = [block_diag(p, p, rk_ref[k]) for k in range(l)]
    zero_bd = jnp.zeros((LANES, LANES), BF16)
    for j in range(0, l, 2):
        lhs = jnp.concatenate(u16[:j + 2] + [s_in], axis=1)
        rhs_rows = [jnp.concatenate([k_bd[j - i] if i <= j else zero_bd, k_bd[j + 1 - i]], axis=1)
                    for i in range(j + 2)]
        rhs_rows.append(jnp.concatenate([block_diag(nst, p, rp_ref[j]), block_diag(nst, p, rp_ref[j + 1])], axis=1))
        y = _dot(lhs, jnp.concatenate(rhs_rows, axis=0))
        for jj in range(2):
            yj = y[:, jj * LANES:(jj + 1) * LANES] + d_ref[...] * u[j + jj]
            y_ref[pl.ds(j + jj, nc, stride=l), :] = jax.nn.gelu(yj)
    o_ref[...] = y_ref[...].astype(o_ref.dtype)


def s5_scan(x, inv_rms, gain, rk, re, rp, s1, s2, d_skip):
    t, d = x.shape
    l, p, gl = S5_CHUNK, S5_GROUP, S5_LANE_GROUPS
    nst = 2 * S5_STATE
    levels = max(1, math.ceil(math.log2(t // l)))
    assert levels <= S5_SCAN_LEVELS

    def per_tile(*shape):
        return pl.BlockSpec((None,) + shape, lambda i: (i,) + (0,) * len(shape))

    col = pl.BlockSpec((t, LANES), lambda i: (0, i))
    vec = pl.BlockSpec((1, LANES), lambda i: (0, i))
    return pl.pallas_call(
        functools.partial(_s5_kernel, levels=levels),
        grid=(d // LANES,),
        in_specs=[col, pl.BlockSpec((t, LANES), lambda i: (0, 0)), vec,
                  per_tile(l, p, LANES), per_tile(l, p, gl * nst), per_tile(l, nst, LANES),
                  per_tile(S5_SCAN_LEVELS, gl * nst), per_tile(S5_SCAN_LEVELS, gl * nst), vec],
        out_specs=col,
        out_shape=jax.ShapeDtypeStruct((t, d), BF16),
        scratch_shapes=[pltpu.VMEM((t, LANES), F32)],
        compiler_params=_params("parallel"),
        name="s5_scan",
    )(x, inv_rms, gain.reshape(1, d), rk, re, rp, s1, s2, d_skip.reshape(1, d))


def s5_mixer(x, inv_rms, gain, a_re, a_im, log_step, b_re, b_im, c_re, c_im, d_skip):
    g, n = a_re.shape
    p, l = S5_GROUP, S5_CHUNK
    gm, emat, pmat, s1, s2 = s5_setup(
        a_re, a_im, log_step, b_re.transpose(0, 2, 1), b_im.transpose(0, 2, 1), c_re, c_im)
    gl, nt, nst = S5_LANE_GROUPS, g // S5_LANE_GROUPS, 2 * n
    rk = gm.reshape(nt, gl, l, p, p).transpose(0, 2, 4, 1, 3).reshape(nt, l, p, gl * p)
    re = emat.reshape(nt, gl, l, p, nst).transpose(0, 2, 3, 1, 4).reshape(nt, l, p, gl * nst)
    rp = pmat.reshape(nt, gl, l, p, nst).transpose(0, 2, 4, 1, 3).reshape(nt, l, nst, gl * p)

    def per_tile(v):
        return v.reshape(nt, gl, S5_SCAN_LEVELS, nst).transpose(0, 2, 1, 3).reshape(nt, S5_SCAN_LEVELS, gl * nst)

    return s5_scan(x, inv_rms, gain, rk, re, rp, per_tile(s1), per_tile(s2), d_skip)


TILES = {"proj": (1024, 512), "gate_up": (2048, 256), "glu": (2048, 256), "down": (512, 512), "out": (1024, 512)}


def _tile(kind, m):
    tm, tn = TILES[kind]
    return dict(tm=min(tm, m), tn=tn)


def _ffn(hn, w_gate, w_up, w_down, layer):
    m = hn.shape[0]
    hidden, w_down16 = matmul_swiglu(hn, Weight(w_gate, layer), Weight(w_up, layer), Weight(w_down, layer),
                                     **_tile("gate_up", m))
    return matmul(hidden, Weight(w_down16), BF16, name="ffn_down", **_tile("down", m))


def kernel(x, norm_g, w_in, gdn_conv_w, gdn_A_log, gdn_dt_bias, gdn_o_norm, fox_f_bias, w_out, s5_A_re, s5_A_im, s5_log_step, s5_B_re, s5_B_im, s5_C_re, s5_C_im, s5_D, s5_w_glu_a, s5_w_glu_b, ffn_w_gate, ffn_w_up, ffn_w_down):
    bsz, t, d = x.shape
    assert bsz == 1
    x = x.reshape(t, d)
    gw = GDN_HEADS * HEAD_DIM
    fw = FOX_HEADS * HEAD_DIM
    proj = _tile("proj", t)

    w0 = w_in[0]
    o_gate = 4 * gw
    o_fox = o_gate + 2 * GDN_HEADS
    o_forget = o_fox + 3 * fw
    w_f = w0[:, o_fox:o_forget].astype(BF16)
    pad = GATE_LANES - 2 * GDN_HEADS - FOX_HEADS
    w_g = jnp.concatenate([w0[:, o_gate:o_fox], w0[:, o_forget:], jnp.zeros((d, pad), F32)], axis=1).astype(BF16)
    prm = jnp.zeros((SUBLANES, GATE_LANES), F32)
    prm = prm.at[0, :GDN_HEADS].set(gdn_A_log[0]).at[1, :GDN_HEADS].set(gdn_dt_bias[0])
    prm = prm.at[2, 2 * GDN_HEADS:2 * GDN_HEADS + FOX_HEADS].set(fox_f_bias[0])

    hn = rmsnorm(x, norm_g[0, 0], BF16)
    proj_a = matmul(hn, Weight(w_in, 0, n=o_gate), F32, name="proj_gdn", **proj)
    fox_q = matmul(hn, Weight(w_f, n=fw), BF16, name="proj_fox_q", scale=FOX_Q_SCALE, **proj)
    fox_kv = matmul(hn, Weight(w_f, col0=fw, n=2 * fw), BF16, name="proj_fox_kv", **proj)
    gates_col = gates(hn, w_g, prm, rows=min(512, t))
    gates_row = gates_col.T.reshape(GATE_LANES, 1, t)
    o_a = gdn_mixer(proj_a, gdn_conv_w[0], gates_col, gates_row, gdn_o_norm[0])
    o_b = fox_attention(fox_q, fox_kv, gates_row, tq=min(512, t))
    m = matmul2(o_a, o_b, Weight(w_out, 0, row_block=0), Weight(w_out, 0, row_block=1), BF16,
                name="out_proj", **_tile("out", t))
    x, hn = resid_norm(x, m, norm_g[0, 1], norm_g[0, 2], BF16)
    f = _ffn(hn, ffn_w_gate, ffn_w_up, ffn_w_down, 0)
    x, inv_rms = resid_scale(x, f, norm_g[0, 3])

    y = s5_mixer(x, inv_rms, norm_g[1, 0], s5_A_re[0], s5_A_im[0], s5_log_step[0], s5_B_re[0], s5_B_im[0],
                 s5_C_re[0], s5_C_im[0], s5_D[0])
    m = matmul_glu(y, Weight(s5_w_glu_a, 0), Weight(s5_w_glu_b, 0), **_tile("glu", t))
    x, hn = resid_norm(x, m, norm_g[1, 1], norm_g[1, 2], BF16)
    f = _ffn(hn, ffn_w_gate, ffn_w_up, ffn_w_down, 1)
    x = resid_norm(x, f, norm_g[1, 3], None, None)
    return x.reshape(bsz, t, d)
```
